```python
import math
import jax, jax.numpy as jnp
from jax import lax
import numpy as np

D_MODEL = 1024
BATCH = 8
SEQ = 2048
DEPTH = 2
DEC_BATCH = 128
DEC_SEQ = 8
PAST_LEN = 2048
PAGE_SIZE = 128

N_META = 16
N_A_LAYERS = DEPTH // 2
N_B_LAYERS = DEPTH - N_A_LAYERS
RWKV_HEAD = 64
RWKV_HEADS = D_MODEL // RWKV_HEAD
DECAY_LORA = 64
AAA_LORA = 64
GATE_LORA = 128
LNX_EPS = 64e-5
DIFF_HEAD = 64
DIFF_HEADS = D_MODEL // (2 * DIFF_HEAD)
QK_WIDTH = 2 * DIFF_HEADS * DIFF_HEAD
V_WIDTH = DIFF_HEADS * 2 * DIFF_HEAD
Q_BLOCK = 128
SUBLN_EPS = 1e-5
N_EXPERTS = 16
N_GROUPS = 4
EXPERTS_PER_GROUP = N_EXPERTS // N_GROUPS
TOPK_GROUPS = 1
TOP_K = 2
D_EXPERT = 512
LN_EPS = 1e-5
DEEPNORM_ALPHA = (2 * DEPTH) ** 0.25
DEEPNORM_BETA = (8 * DEPTH) ** -0.25

kernel_name = 'yoco_rwkv7_diffattn_grouped_moe_step'


def _layernorm(x, g, b):
    xf = x.astype(jnp.float32)
    mu = xf.mean(-1, keepdims=True)
    var = jnp.square(xf - mu).mean(-1, keepdims=True)
    return ((xf - mu) * lax.rsqrt(var + LN_EPS) * g + b).astype(x.dtype)


def _lambda_init(layer_idx):
    return 0.8 - 0.6 * math.exp(-0.3 * layer_idx)


def _rwkv7_step(S, inp):
    r, w, k, v, a, b = inp
    sa = jnp.einsum('bhij,bhj->bhi', S, a)
    S = S * w[:, :, None, :] + sa[..., None] * b[:, :, None, :] + v[..., None] * k[:, :, None, :]
    return S, jnp.einsum('bhij,bhj->bhi', S, r)


def _rwkv7_time_mix(x, wkv0, x_prev, mix, w_rkv, dw0, dw1, dw2, aw0, aw1, aw2, gw1, gw2, kk_ka, r_k, lnx, w_out):
    B, T, D = x.shape
    H, N = RWKV_HEADS, RWKV_HEAD
    f32 = jnp.float32
    xx = jnp.concatenate([x_prev[:, None, :].astype(x.dtype), x[:, :-1]], axis=1) - x
    xr, xw, xk, xv, xa, xg = (x + xx * mix[i] for i in range(6))
    r = xr @ w_rkv[0]
    k = xk @ w_rkv[1]
    v = xv @ w_rkv[2]
    w_log = -jax.nn.softplus(-(dw0 + jnp.tanh(xw @ dw1) @ dw2)) - 0.5
    decay = jnp.exp(-jnp.exp(w_log.astype(f32)))
    a = jax.nn.sigmoid(aw0 + (xa @ aw1) @ aw2)
    g = jax.nn.sigmoid(xg @ gw1) @ gw2
    kk = (k * kk_ka[0]).astype(f32).reshape(B, T, H, N)
    kk = kk / jnp.maximum(jnp.linalg.norm(kk, axis=-1, keepdims=True), 1e-12)
    k = k * (1 + (a - 1) * kk_ka[1])

    def heads(t):
        return t.astype(f32).reshape(B, T, H, N)

    def tmaj(t):
        return jnp.swapaxes(t, 0, 1)

    rh, kh, vh, ah = heads(r), heads(k), heads(v), heads(a)
    S_fin, o = lax.scan(_rwkv7_step, wkv0.astype(f32),
                        (tmaj(rh), tmaj(heads(decay)), tmaj(kh), tmaj(vh), tmaj(-kk), tmaj(kk * ah)))
    o = tmaj(o)
    mu = o.mean(-1, keepdims=True)
    var = jnp.square(o - mu).mean(-1, keepdims=True)
    o = ((o - mu) * lax.rsqrt(var + LNX_EPS)).reshape(B, T, D) * lnx[0] + lnx[1]
    o = o + (jnp.sum(rh * kh * r_k, -1, keepdims=True) * vh).reshape(B, T, D)
    out = (o.astype(x.dtype) * g) @ w_out
    return out, S_fin.astype(wkv0.dtype), x[:, -1]


def _shared_kv(x, w_kv):
    B, T, _ = x.shape
    kv = x @ w_kv
    k = kv[..., :QK_WIDTH].reshape(B, T, 2 * DIFF_HEADS, DIFF_HEAD)
    v = kv[..., QK_WIDTH:].reshape(B, T, DIFF_HEADS, 2 * DIFF_HEAD)
    return k, v


def _diff_attend(q, k, v, q_offset, lam):
    B, Tq = q.shape[0], q.shape[1]
    S = k.shape[1]
    qblk = min(Q_BLOCK, Tq)
    nb = -(-Tq // qblk)
    qb = jnp.pad(q, ((0, 0), (0, nb * qblk - Tq), (0, 0), (0, 0)))
    qb = qb.reshape(B, nb, qblk, 2 * DIFF_HEADS, DIFF_HEAD).swapaxes(0, 1)
    key_pos = jnp.arange(S)

    def block(args):
        qi, start = args
        qpos = q_offset + start + jnp.arange(qblk)
        s = jnp.einsum('bqhd,bkhd->bhqk', qi, k).astype(jnp.float32) * (DIFF_HEAD ** -0.5)
        s = jnp.where(key_pos[None, :] <= qpos[:, None], s, -jnp.inf)
        p = jax.nn.softmax(s, axis=-1).reshape(B, DIFF_HEADS, 2, qblk, S)
        amap = p[:, :, 0] - lam * p[:, :, 1]
        return jnp.einsum('bhqk,bkhe->bqhe', amap.astype(v.dtype), v)

    o = lax.map(block, (qb, jnp.arange(nb) * qblk))
    return o.swapaxes(0, 1).reshape(B, nb * qblk, DIFF_HEADS, 2 * DIFF_HEAD)[:, :Tq]


def _diff_attn_layer(x, k_all, v_all, w_q, lam_vecs, subln, w_out, lambda_init):
    B, T, _ = x.shape
    q = (x @ w_q).reshape(B, T, 2 * DIFF_HEADS, DIFF_HEAD)
    lf = lam_vecs.astype(jnp.float32)
    lam = jnp.exp(jnp.sum(lf[0] * lf[1])) - jnp.exp(jnp.sum(lf[2] * lf[3])) + lambda_init
    o = _diff_attend(q, k_all, v_all, k_all.shape[1] - T, lam).astype(jnp.float32)
    o = o * lax.rsqrt(jnp.mean(jnp.square(o), -1, keepdims=True) + SUBLN_EPS) * subln
    o = o * (1.0 - lambda_init)
    return o.reshape(B, T, D_MODEL).astype(x.dtype) @ w_out


def _moe(x, router_w, router_b, w_gate, w_up, w_down):
    B, T, D = x.shape
    f32 = jnp.float32
    xt = x.reshape(B * T, D)
    s = jax.nn.sigmoid((xt @ router_w).astype(f32))
    sel = s + router_b.astype(f32)
    grp = lax.top_k(sel.reshape(-1, N_GROUPS, EXPERTS_PER_GROUP), 2)[0].sum(-1)
    _, gidx = lax.top_k(grp, TOPK_GROUPS)
    gmask = jax.nn.one_hot(gidx, N_GROUPS, dtype=f32).sum(1)
    emask = jnp.repeat(gmask, EXPERTS_PER_GROUP, axis=-1) > 0
    _, eidx = lax.top_k(jnp.where(emask, sel, -jnp.inf), TOP_K)
    wts = jnp.take_along_axis(s, eidx, -1)
    wts = wts / wts.sum(-1, keepdims=True)
    gates = jnp.einsum('nk,nke->ne', wts, jax.nn.one_hot(eidx, N_EXPERTS, dtype=f32)).astype(x.dtype)
    y = jnp.zeros_like(xt)
    for e in range(N_EXPERTS):
        h = jax.nn.silu(xt @ w_gate[e]) * (xt @ w_up[e])
        y = y + gates[:, e:e + 1] * (h @ w_down[e])
    return y.reshape(B, T, D)


def setup_inputs(seed: int = 0) -> dict:
    key = jax.random.key(seed)
    ks = jax.random.split(key, 40)
    f32 = jnp.float32
    D, H, N = D_MODEL, RWKV_HEADS, RWKV_HEAD
    NA, NB = N_A_LAYERS, N_B_LAYERS
    n_pages = PAST_LEN // PAGE_SIZE
    n_used = DEC_BATCH * n_pages
    n_pool = n_used + n_used // 4

    def nrm(i, shape, scale):
        return jax.random.normal(ks[i], shape, f32) * scale

    page_table = jax.random.permutation(ks[4], n_pool)[:n_used].reshape(DEC_BATCH, n_pages).astype(jnp.int32)
    w_rkv = nrm(8, (NA, 3, D, D), D ** -0.5) * jnp.array([1.0, 1.0, DEEPNORM_BETA], f32)[None, :, None, None]
    w_kv = nrm(22, (D, QK_WIDTH + V_WIDTH), D ** -0.5) * jnp.concatenate(
        [jnp.ones((QK_WIDTH,), f32), jnp.full((V_WIDTH,), DEEPNORM_BETA, f32)])
    post_ln = jnp.stack([1.0 + nrm(33, (DEPTH, 2, D), 0.05), nrm(34, (DEPTH, 2, D), 0.01)], axis=2)
    return {
        'x_prompt': nrm(0, (BATCH, SEQ, D), 1.0),
        'x_sample': nrm(1, (DEC_BATCH, DEC_SEQ, D), 1.0),
        'cache_k': nrm(2, (n_pool, PAGE_SIZE, 2 * DIFF_HEADS, DIFF_HEAD), 1.0),
        'cache_v': nrm(3, (n_pool, PAGE_SIZE, DIFF_HEADS, 2 * DIFF_HEAD), 1.0),
        'page_table': page_table,
        'state_wkv': nrm(5, (DEC_BATCH, NA, H, N, N), 0.3),
        'state_shift': nrm(6, (DEC_BATCH, NA, D), 1.0),
        'meta_tokens': nrm(7, (N_META, D), 1.0),
        'rwkv_mix': jax.random.uniform(ks[9], (NA, 6, D), f32),
        'rwkv_w_rkv': w_rkv,
        'rwkv_decay_w0': jax.random.uniform(ks[10], (NA, D), f32, -5.0, -1.0),
        'rwkv_decay_w1': nrm(11, (NA, D, DECAY_LORA), D ** -0.5),
        'rwkv_decay_w2': nrm(12, (NA, DECAY_LORA, D), 0.3 * DECAY_LORA ** -0.5),
        'rwkv_a_w0': nrm(13, (NA, D), 0.1),
        'rwkv_a_w1': nrm(14, (NA, D, AAA_LORA), D ** -0.5),
        'rwkv_a_w2': nrm(15, (NA, AAA_LORA, D), 0.5 * AAA_LORA ** -0.5),
        'rwkv_g_w1': nrm(16, (NA, D, GATE_LORA), D ** -0.5),
        'rwkv_g_w2': nrm(17, (NA, GATE_LORA, D), GATE_LORA ** -0.5),
        'rwkv_kk_ka': jnp.stack([0.85 + nrm(18, (NA, D), 0.05), 1.0 + nrm(19, (NA, D), 0.05)], axis=1),
        'rwkv_r_k': nrm(20, (NA, H, N), 0.1),
        'rwkv_lnx': jnp.stack([1.0 + nrm(21, (NA, D), 0.05), nrm(35, (NA, D), 0.01)], axis=1),
        'rwkv_w_out': nrm(23, (NA, D, D), D ** -0.5 * DEEPNORM_BETA),
        'w_kv_shared': w_kv,
        'diff_w_q': nrm(24, (NB, D, QK_WIDTH), D ** -0.5),
        'diff_lambda': nrm(25, (NB, 4, DIFF_HEAD), 0.1),
        'diff_subln': 1.0 + nrm(26, (NB, 2 * DIFF_HEAD), 0.05),
        'diff_w_out': nrm(27, (NB, V_WIDTH, D), V_WIDTH ** -0.5 * DEEPNORM_BETA),
        'router_w': nrm(28, (D, N_EXPERTS), D ** -0.5),
        'router_b': nrm(29, (N_EXPERTS,), 0.01),
        'moe_w_gate': nrm(30, (DEPTH, N_EXPERTS, D, D_EXPERT), D ** -0.5),
        'moe_w_up': nrm(31, (DEPTH, N_EXPERTS, D, D_EXPERT), D ** -0.5),
        'moe_w_down': nrm(32, (DEPTH, N_EXPERTS, D_EXPERT, D), D_EXPERT ** -0.5 * DEEPNORM_BETA),
        'post_ln': post_ln,
    }


def reference(x_prompt, x_sample, cache_k, cache_v, page_table, state_wkv, state_shift, meta_tokens,
              rwkv_mix, rwkv_w_rkv, rwkv_decay_w0, rwkv_decay_w1, rwkv_decay_w2, rwkv_a_w0, rwkv_a_w1,
              rwkv_a_w2, rwkv_g_w1, rwkv_g_w2, rwkv_kk_ka, rwkv_r_k, rwkv_lnx, rwkv_w_out, w_kv_shared,
              diff_w_q, diff_lambda, diff_subln, diff_w_out, router_w, router_b, moe_w_gate, moe_w_up,
              moe_w_down, post_ln):
    def run(x, wkv_in, shift_in, k_past, v_past):
        new_wkv, new_shift = [], []
        k_new = v_new = k_all = v_all = None
        for l in range(DEPTH):
            if l < N_A_LAYERS:
                h, s_out, x_last = _rwkv7_time_mix(
                    x, wkv_in[:, l], shift_in[:, l], rwkv_mix[l], rwkv_w_rkv[l], rwkv_decay_w0[l],
                    rwkv_decay_w1[l], rwkv_decay_w2[l], rwkv_a_w0[l], rwkv_a_w1[l], rwkv_a_w2[l],
                    rwkv_g_w1[l], rwkv_g_w2[l], rwkv_kk_ka[l], rwkv_r_k[l], rwkv_lnx[l], rwkv_w_out[l])
                new_wkv.append(s_out)
                new_shift.append(x_last)
            else:
                j = l - N_A_LAYERS
                h = _diff_attn_layer(x, k_all, v_all, diff_w_q[j], diff_lambda[j], diff_subln[j],
                                     diff_w_out[j], _lambda_init(l))
            x = _layernorm(DEEPNORM_ALPHA * x + h, post_ln[l, 0, 0], post_ln[l, 0, 1])
            x = _layernorm(DEEPNORM_ALPHA * x + _moe(x, router_w, router_b, moe_w_gate[l], moe_w_up[l], moe_w_down[l]),
                           post_ln[l, 1, 0], post_ln[l, 1, 1])
            if l == N_A_LAYERS - 1:
                k_new, v_new = _shared_kv(x, w_kv_shared)
                if k_past is None:
                    k_all, v_all = k_new, v_new
                else:
                    k_all = jnp.concatenate([k_past, k_new.astype(k_past.dtype)], axis=1)
                    v_all = jnp.concatenate([v_past, v_new.astype(v_past.dtype)], axis=1)
        return x, k_new, v_new, jnp.stack(new_wkv, axis=1), jnp.stack(new_shift, axis=1)

    B = x_prompt.shape[0]
    meta = jnp.broadcast_to(meta_tokens[None].astype(x_prompt.dtype), (B, N_META, D_MODEL))
    xp = jnp.concatenate([meta, x_prompt], axis=1)
    wkv0 = jnp.zeros((B, N_A_LAYERS, RWKV_HEADS, RWKV_HEAD, RWKV_HEAD), x_prompt.dtype)
    shift0 = jnp.zeros((B, N_A_LAYERS, D_MODEL), x_prompt.dtype)
    yp, new_k_prompt, new_v_prompt, new_wkv_prompt, new_shift_prompt = run(xp, wkv0, shift0, None, None)
    y_prompt = yp[:, N_META:]

    DB, n_pages = page_table.shape
    past = n_pages * cache_k.shape[1]
    k_past = cache_k[page_table].reshape(DB, past, 2 * DIFF_HEADS, DIFF_HEAD)
    v_past = cache_v[page_table].reshape(DB, past, DIFF_HEADS, 2 * DIFF_HEAD)
    y_sample, new_k_sample, new_v_sample, new_wkv_sample, new_shift_sample = run(
        x_sample, state_wkv, state_shift, k_past, v_past)
    return (y_prompt, y_sample, new_k_prompt, new_v_prompt, new_wkv_prompt, new_shift_prompt,
            new_k_sample, new_v_sample, new_wkv_sample, new_shift_sample)
```

```python
import functools
import math

import jax
import jax.numpy as jnp
from jax import lax
from jax.experimental import pallas as pl
from jax.experimental.pallas import tpu as pltpu

F32 = jnp.float32
BF16 = jnp.bfloat16

DEPTH = 2
N_A_LAYERS = 1
RWKV_HEAD = 64
DIFF_HEAD = 64
LNX_EPS = 64e-5
SUBLN_EPS = 1e-5
LN_EPS = 1e-5
N_EXPERTS = 16
N_GROUPS = 4
EXPERTS_PER_GROUP = N_EXPERTS // N_GROUPS
TOP_K = 2
DEEPNORM_ALPHA = (2 * DEPTH) ** 0.25
LAMBDA_INIT = 0.8 - 0.6 * math.exp(-0.3 * 1)

LANES = 128
SUBLANES = 8
VMEM_LIMIT_BYTES = 56 * 1024 * 1024


def _cparams(*sem):
    return pltpu.CompilerParams(dimension_semantics=sem, vmem_limit_bytes=VMEM_LIMIT_BYTES)


def _row_tile(n, candidates):
    for c in candidates:
        if n % c == 0:
            return c
    raise ValueError(f"no row tile for {n} in {candidates}")


def _full(shape):
    zeros = (0,) * len(shape)
    return pl.BlockSpec(shape, lambda *_: zeros)


def _dot(a, b):
    return jnp.dot(a, b, preferred_element_type=F32)


def _dot_nt(a, b):
    return lax.dot_general(a, b, (((1,), (1,)), ((), ())), preferred_element_type=F32)


def _split_dot(x, w_bf16):
    hi = x.astype(BF16)
    lo = (x - hi.astype(F32)).astype(BF16)
    return _dot(hi, w_bf16) + _dot(lo, w_bf16)


def _layernorm(z, g, b):
    mu = jnp.mean(z, axis=-1, keepdims=True)
    zc = z - mu
    var = jnp.mean(zc * zc, axis=-1, keepdims=True)
    return zc * lax.rsqrt(var + LN_EPS) * g + b


def _head_indicator(d, head):
    n_heads = d // head
    assert n_heads <= LANES
    ind = (jnp.arange(d)[:, None] // head == jnp.arange(LANES)[None, :]).astype(BF16)
    return ind, ind.T


def _rwkv_pre_kernel(x_ref, xp_ref, mix_ref, wr_ref, wk_ref, wv_ref, dw0_ref, dw1_ref, dw2_ref,
                     aw0_ref, aw1_ref, aw2_ref, gw1_ref, gw2_ref, kkka_ref, ind_ref, indt_ref,
                     r_ref, w_ref, k_ref, v_ref, a_ref, b_ref, g_ref):
    x = x_ref[...]
    xx = xp_ref[...] - x

    def mixed(i):
        return (x + xx * mix_ref[i:i + 1, :]).astype(BF16)

    r = _dot(mixed(0), wr_ref[...])
    k = _dot(mixed(2), wk_ref[...])
    v = _dot(mixed(3), wv_ref[...])
    lw = jnp.tanh(_dot(mixed(1), dw1_ref[...])).astype(BF16)
    z = -(dw0_ref[...] + _dot(lw, dw2_ref[...]))
    softplus = jnp.maximum(z, 0.0) + jnp.log(1.0 + jnp.exp(-jnp.abs(z)))
    w_log = -softplus - 0.5
    decay = jnp.exp(-jnp.exp(w_log))
    la = _dot(mixed(4), aw1_ref[...]).astype(BF16)
    a = jax.nn.sigmoid(aw0_ref[...] + _dot(la, aw2_ref[...]))
    lg = jax.nn.sigmoid(_dot(mixed(5), gw1_ref[...])).astype(BF16)
    g = _dot(lg, gw2_ref[...])

    kk = k * kkka_ref[0:1, :]
    ss = _split_dot(_split_dot(kk * kk, ind_ref[...]), indt_ref[...])
    kk = kk / jnp.maximum(jnp.sqrt(ss), 1e-12)
    k = k * (1.0 + (a - 1.0) * kkka_ref[1:2, :])

    r_ref[...] = r
    w_ref[...] = decay
    k_ref[...] = k
    v_ref[...] = v
    a_ref[...] = -kk
    b_ref[...] = kk * a
    g_ref[...] = g


def _rwkv_pre(x, x_prev, p):
    n, d = x.shape
    tm = _row_tile(n, (192, 128, 64, 8))
    row = pl.BlockSpec((tm, d), lambda i: (i, 0))
    weights = [p["mix"], p["w_r"], p["w_k"], p["w_v"], p["dw0"], p["dw1"], p["dw2"], p["aw0"],
               p["aw1"], p["aw2"], p["gw1"], p["gw2"], p["kk_ka"], p["ind"], p["ind_t"]]
    return pl.pallas_call(
        _rwkv_pre_kernel,
        grid=(n // tm,),
        in_specs=[row, row] + [_full(w.shape) for w in weights],
        out_specs=[row] * 7,
        out_shape=[jax.ShapeDtypeStruct((n, d), F32)] * 7,
        compiler_params=_cparams("parallel"),
        name="rwkv_pre",
    )(x, x_prev, *weights)


def _rwkv_scan_kernel(r_ref, w_ref, k_ref, v_ref, a_ref, b_ref, s0_ref, o_ref, sout_ref, s_ref):
    tc = pl.program_id(1)
    n_tc = pl.num_programs(1)
    steps, nk, _ = r_ref.shape

    @pl.when(tc == 0)
    def _():
        s_ref[...] = s0_ref[...]

    def step(t, carry):
        vt = v_ref[t]
        sa = jnp.zeros_like(vt)
        for j in range(nk):
            sa = sa + s_ref[j] * a_ref[t, pl.ds(j, 1), :]
        ot = jnp.zeros_like(vt)
        for j in range(nk):
            s = (s_ref[j] * w_ref[t, pl.ds(j, 1), :] + sa * b_ref[t, pl.ds(j, 1), :]
                 + vt * k_ref[t, pl.ds(j, 1), :])
            s_ref[j] = s
            ot = ot + s * r_ref[t, pl.ds(j, 1), :]
        o_ref[t] = ot
        return carry

    lax.fori_loop(0, steps, step, 0)

    @pl.when(tc == n_tc - 1)
    def _():
        sout_ref[...] = s_ref[...]


def _rwkv_scan(r, w, k, v, a, b, s0):
    t, nk, l = r.shape
    steps = _row_tile(t, (16, 8, 4, 2, 1))
    seq = pl.BlockSpec((steps, nk, LANES), lambda g, c: (c, 0, g))
    st = pl.BlockSpec((nk, nk, LANES), lambda g, c: (0, 0, g))
    return pl.pallas_call(
        _rwkv_scan_kernel,
        grid=(l // LANES, t // steps),
        in_specs=[seq] * 6 + [st],
        out_specs=[seq, st],
        out_shape=[jax.ShapeDtypeStruct((t, nk, l), F32), jax.ShapeDtypeStruct((nk, nk, l), F32)],
        scratch_shapes=[pltpu.VMEM((nk, nk, LANES), F32)],
        compiler_params=_cparams("parallel", "arbitrary"),
        name="rwkv_scan",
    )(r, w, k, v, a, b, s0)


def _rwkv_post_kernel(o_ref, r_ref, k_ref, v_ref, g_ref, x_ref, lnx_ref, rk_ref, ind_ref, indt_ref,
                      wout_ref, ln_ref, out_ref):
    ind = ind_ref[...]
    ind_t = indt_ref[...]
    inv_head = 1.0 / RWKV_HEAD
    o = o_ref[...]
    mu = _split_dot(_split_dot(o, ind) * inv_head, ind_t)
    oc = o - mu
    var = _split_dot(oc * oc, ind) * inv_head
    rstd = _split_dot(lax.rsqrt(var + LNX_EPS), ind_t)
    on = oc * rstd * lnx_ref[0:1, :] + lnx_ref[1:2, :]
    v = v_ref[...]
    bonus = _split_dot(_split_dot(r_ref[...] * k_ref[...] * rk_ref[...], ind), ind_t)
    on = on + bonus * v
    h = _dot((on * g_ref[...]).astype(BF16), wout_ref[...])
    out_ref[...] = _layernorm(DEEPNORM_ALPHA * x_ref[...] + h, ln_ref[0:1, :], ln_ref[1:2, :])


def _rwkv_post(o, r, k, v, g, x, p, ln):
    n, d = x.shape
    tm = _row_tile(n, (384, 256, 128, 64, 8))
    row = pl.BlockSpec((tm, d), lambda i: (i, 0))
    weights = [p["lnx"], p["r_k"], p["ind"], p["ind_t"], p["w_out"], ln]
    return pl.pallas_call(
        _rwkv_post_kernel,
        grid=(n // tm,),
        in_specs=[row] * 6 + [_full(w.shape) for w in weights],
        out_specs=row,
        out_shape=jax.ShapeDtypeStruct((n, d), F32),
        compiler_params=_cparams("parallel"),
        name="rwkv_post",
    )(o, r, k, v, g, x, *weights)


def _proj_kernel(*refs, n_out):
    x_ref = refs[0]
    w_refs = refs[1:1 + n_out]
    out_refs = refs[1 + n_out:]
    x = x_ref[...].astype(BF16)
    for w_ref, out_ref in zip(w_refs, out_refs):
        out_ref[...] = _dot(x, w_ref[...])


def _proj(x, ws, name):
    n, d = x.shape
    tm = _row_tile(n, (384, 256, 128, 64, 8))
    return pl.pallas_call(
        functools.partial(_proj_kernel, n_out=len(ws)),
        grid=(n // tm,),
        in_specs=[pl.BlockSpec((tm, d), lambda i: (i, 0))] + [_full(w.shape) for w in ws],
        out_specs=[pl.BlockSpec((tm, w.shape[1]), lambda i: (i, 0)) for w in ws],
        out_shape=[jax.ShapeDtypeStruct((n, w.shape[1]), F32) for w in ws],
        compiler_params=_cparams("parallel"),
        name=name,
    )(x, *ws)


def _proj_ln_kernel(a_ref, x_ref, w_ref, ln_ref, out_ref):
    h = _dot(a_ref[...].astype(BF16), w_ref[...])
    out_ref[...] = _layernorm(DEEPNORM_ALPHA * x_ref[...] + h, ln_ref[0:1, :], ln_ref[1:2, :])


def _proj_ln(a, x, w, ln):
    n, d = x.shape
    tm = _row_tile(n, (384, 256, 128, 64, 8))
    row = pl.BlockSpec((tm, d), lambda i: (i, 0))
    return pl.pallas_call(
        _proj_ln_kernel,
        grid=(n // tm,),
        in_specs=[pl.BlockSpec((tm, a.shape[1]), lambda i: (i, 0)), row, _full(w.shape), _full(ln.shape)],
        out_specs=row,
        out_shape=jax.ShapeDtypeStruct((n, d), F32),
        compiler_params=_cparams("parallel"),
        name="proj_ln",
    )(a, x, w, ln)


def _combine_ln_kernel(y_ref, x_ref, ln_ref, out_ref):
    y = y_ref[0] + y_ref[1]
    out_ref[...] = _layernorm(DEEPNORM_ALPHA * x_ref[...] + y, ln_ref[0:1, :], ln_ref[1:2, :])


def _combine_ln(y2, x, ln):
    n, d = x.shape
    tm = _row_tile(n, (384, 256, 128, 64, 8))
    row = pl.BlockSpec((tm, d), lambda i: (i, 0))
    return pl.pallas_call(
        _combine_ln_kernel,
        grid=(n // tm,),
        in_specs=[pl.BlockSpec((TOP_K, tm, d), lambda i: (0, i, 0)), row, _full(ln.shape)],
        out_specs=row,
        out_shape=jax.ShapeDtypeStruct((n, d), F32),
        compiler_params=_cparams("parallel"),
        name="combine_ln",
    )(y2, x, ln)


def _router_kernel(x_ref, wt_ref, bias_ref, eidx_ref, wts_ref):
    x = x_ref[...]
    x_hi = x.astype(BF16)
    x_lo = (x - x_hi.astype(F32)).astype(BF16)
    wt = wt_ref[...]
    wt_hi = wt.astype(BF16)
    wt_lo = (wt - wt_hi.astype(F32)).astype(BF16)
    logits = _dot_nt(wt_hi, x_hi) + _dot_nt(wt_hi, x_lo) + _dot_nt(wt_lo, x_hi)
    s = jax.nn.sigmoid(logits)
    sel = s + bias_ref[...]
    s_rows = [s[e:e + 1, :] for e in range(N_EXPERTS)]
    sel_rows = [sel[e:e + 1, :] for e in range(N_EXPERTS)]

    best_grp = None
    best_idx = None
    for gi in range(N_GROUPS):
        rows = sel_rows[gi * EXPERTS_PER_GROUP:(gi + 1) * EXPERTS_PER_GROUP]
        score = None
        for i in range(EXPERTS_PER_GROUP):
            for j in range(i + 1, EXPERTS_PER_GROUP):
                pair = rows[i] + rows[j]
                score = pair if score is None else jnp.maximum(score, pair)
        if best_grp is None:
            best_grp = score
            best_idx = jnp.zeros(score.shape, jnp.int32)
        else:
            better = score > best_grp
            best_grp = jnp.where(better, score, best_grp)
            best_idx = jnp.where(better, gi, best_idx)

    neg_inf = jnp.float32(-jnp.inf)
    masked = [jnp.where(best_idx == e // EXPERTS_PER_GROUP, sel_rows[e], neg_inf)
              for e in range(N_EXPERTS)]

    def argmax_rows(exclude):
        top = jnp.full(masked[0].shape, neg_inf)
        idx = jnp.full(masked[0].shape, -1, jnp.int32)
        val = jnp.zeros(masked[0].shape, F32)
        for e in range(N_EXPERTS):
            cand = masked[e] if exclude is None else jnp.where(exclude == e, neg_inf, masked[e])
            better = cand > top
            top = jnp.where(better, cand, top)
            idx = jnp.where(better, e, idx)
            val = jnp.where(better, s_rows[e], val)
        return idx, val

    i1, w1 = argmax_rows(None)
    i2, w2 = argmax_rows(i1)
    tot = w1 + w2
    eidx_ref[0:1, :] = i1
    eidx_ref[1:2, :] = i2
    wts_ref[0:1, :] = w1 / tot
    wts_ref[1:2, :] = w2 / tot


def _router(x, router_wt, router_b):
    n, d = x.shape
    tm = _row_tile(n, (384, 256, 128))
    return pl.pallas_call(
        _router_kernel,
        grid=(n // tm,),
        in_specs=[pl.BlockSpec((tm, d), lambda i: (i, 0)), _full(router_wt.shape), _full(router_b.shape)],
        out_specs=[pl.BlockSpec((TOP_K, tm), lambda i: (0, i))] * 2,
        out_shape=[jax.ShapeDtypeStruct((TOP_K, n), jnp.int32), jax.ShapeDtypeStruct((TOP_K, n), F32)],
        compiler_params=_cparams("parallel"),
        name="router",
    )(x, router_wt, router_b)


def _moe_kernel(tile_e_ref, n_used_ref, src_ref, dst_ref, gate_ref, x_hbm, wg_ref, wu_ref, wd_ref,
                y_hbm, xbuf, ybuf, sem_in, sem_out):
    del tile_e_ref
    i = pl.program_id(0)
    tm = xbuf.shape[0]

    def gather_copy(r):
        tok = src_ref[0, 0, r]
        return pltpu.make_async_copy(x_hbm.at[pl.ds(tok, 1)], xbuf.at[pl.ds(r, 1)], sem_in)

    def scatter_copy(r):
        row = dst_ref[0, 0, r]
        return pltpu.make_async_copy(ybuf.at[pl.ds(r, 1)], y_hbm.at[pl.ds(row, 1)], sem_out)

    @pl.when(i < n_used_ref[0])
    def _():
        def start_in(r, c):
            gather_copy(r).start()
            return c
        lax.fori_loop(0, tm, start_in, 0)

        def wait_in(r, c):
            gather_copy(r).wait()
            return c
        lax.fori_loop(0, tm, wait_in, 0)

        x = xbuf[...].astype(BF16)
        hg = _dot(x, wg_ref[0])
        hu = _dot(x, wu_ref[0])
        h = (hg * jax.nn.sigmoid(hg) * hu).astype(BF16)
        ybuf[...] = gate_ref[...] * _dot(h, wd_ref[0])

        def start_out(r, c):
            @pl.when(dst_ref[0, 0, r] >= 0)
            def _():
                scatter_copy(r).start()
            return c
        lax.fori_loop(0, tm, start_out, 0)

        def wait_out(r, c):
            @pl.when(dst_ref[0, 0, r] >= 0)
            def _():
                scatter_copy(r).wait()
            return c
        lax.fori_loop(0, tm, wait_out, 0)


def _moe(x, router_wt, router_b, wg, wu, wd):
    n, d = x.shape
    n_exp, _, d_exp = wg.shape
    eidx, wts = _router(x, router_wt, router_b)

    tm = 256
    n_pairs = TOP_K * n
    n_tiles = pl.cdiv(n_pairs, tm) + n_exp
    e_flat = eidx.reshape(n_pairs)
    onehot = (e_flat[:, None] == jnp.arange(n_exp, dtype=jnp.int32)[None, :]).astype(jnp.int32)
    csum = jnp.cumsum(onehot, axis=0)
    counts = csum[-1]
    rank = jnp.sum(onehot * (csum - 1), axis=1)
    tiles_per_e = (counts + tm - 1) // tm
    tile_end = jnp.cumsum(tiles_per_e)
    tile_start = tile_end - tiles_per_e
    n_used = tile_end[-1]
    pos = tile_start[e_flat] * tm + rank
    pair_ids = jnp.arange(n_pairs, dtype=jnp.int32)
    src = jnp.zeros((n_tiles * tm,), jnp.int32).at[pos].set(pair_ids % n)
    dst = jnp.full((n_tiles * tm,), -1, jnp.int32).at[pos].set(pair_ids)
    gate = jnp.zeros((n_tiles * tm,), F32).at[pos].set(wts.reshape(n_pairs))
    tile_ids = jnp.arange(n_tiles, dtype=jnp.int32)
    tile_e = jnp.searchsorted(tile_end, jnp.minimum(tile_ids, n_used - 1), side="right").astype(jnp.int32)
    tile_e = jnp.minimum(tile_e, n_exp - 1)

    idx_spec = pl.BlockSpec((1, 1, tm), lambda i, te, nu: (i, 0, 0), memory_space=pltpu.SMEM)
    y = pl.pallas_call(
        _moe_kernel,
        grid_spec=pltpu.PrefetchScalarGridSpec(
            num_scalar_prefetch=2,
            grid=(n_tiles,),
            in_specs=[
                idx_spec,
                idx_spec,
                pl.BlockSpec((tm, 1), lambda i, te, nu: (i, 0)),
                pl.BlockSpec(memory_space=pl.ANY),
                pl.BlockSpec((1, d, d_exp), lambda i, te, nu: (te[i], 0, 0)),
                pl.BlockSpec((1, d, d_exp), lambda i, te, nu: (te[i], 0, 0)),
                pl.BlockSpec((1, d_exp, d), lambda i, te, nu: (te[i], 0, 0)),
            ],
            out_specs=pl.BlockSpec(memory_space=pl.ANY),
            scratch_shapes=[
                pltpu.VMEM((tm, d), F32),
                pltpu.VMEM((tm, d), F32),
                pltpu.SemaphoreType.DMA(()),
                pltpu.SemaphoreType.DMA(()),
            ],
        ),
        out_shape=jax.ShapeDtypeStruct((n_pairs, d), F32),
        compiler_params=_cparams("arbitrary"),
        name="moe_experts",
    )(tile_e, n_used.reshape(1).astype(jnp.int32), src.reshape(n_tiles, 1, tm), dst.reshape(n_tiles, 1, tm),
      gate.reshape(n_tiles * tm, 1), x, wg, wu, wd)
    return y.reshape(TOP_K, n, d)


def _lambda_value(lam_ref):
    lf = lam_ref[...]
    s01 = jnp.sum(lf[0:1, :] * lf[1:2, :], axis=-1, keepdims=True)
    s23 = jnp.sum(lf[2:3, :] * lf[3:4, :], axis=-1, keepdims=True)
    return jnp.exp(s01) - jnp.exp(s23) + LAMBDA_INIT


def _sub_ln(o, subln):
    o = o * lax.rsqrt(jnp.mean(o * o, axis=-1, keepdims=True) + SUBLN_EPS) * subln
    return o * (1.0 - LAMBDA_INIT)


def _split_maps(q):
    lane = lax.broadcasted_iota(jnp.int32, q.shape, 1)
    q0 = jnp.where(lane < DIFF_HEAD, q, 0.0).astype(BF16)
    q1 = jnp.where(lane >= DIFF_HEAD, q, 0.0).astype(BF16)
    return q0, q1


def _attn_prompt_kernel(lam_ref, subln_ref, q_ref, k_ref, v_ref, o_ref):
    qi = pl.program_id(2)
    tq = q_ref.shape[1]
    q0, q1 = _split_maps(q_ref[0] * (DIFF_HEAD ** -0.5))
    row_pos = qi * tq + lax.broadcasted_iota(jnp.int32, (tq, tq), 0)
    col_iota = lax.broadcasted_iota(jnp.int32, (tq, tq), 1)

    def chunk(j, carry):
        start = pl.multiple_of(j * tq, SUBLANES)
        kc = k_ref[0, pl.ds(start, tq), :].astype(BF16)
        vc = v_ref[0, pl.ds(start, tq), :].astype(BF16)
        visible = (j * tq + col_iota) <= row_pos
        out = []
        for qm, (m, l, acc) in zip((q0, q1), carry):
            s = jnp.where(visible, _dot_nt(qm, kc), -jnp.inf)
            m_new = jnp.maximum(m, jnp.max(s, axis=-1, keepdims=True))
            p = jnp.exp(s - m_new)
            scale = jnp.exp(m - m_new)
            l_new = scale * l + jnp.sum(p, axis=-1, keepdims=True)
            acc_new = scale * acc + _dot(p.astype(BF16), vc)
            out.append((m_new, l_new, acc_new))
        return tuple(out)

    init = (jnp.full((tq, 1), -jnp.inf, F32), jnp.zeros((tq, 1), F32), jnp.zeros((tq, 2 * DIFF_HEAD), F32))
    (_, l0, a0), (_, l1, a1) = lax.fori_loop(0, qi + 1, chunk, (init, init))
    o = a0 / l0 - _lambda_value(lam_ref) * (a1 / l1)
    o_ref[0] = _sub_ln(o, subln_ref[...])


def _attn_prompt(q, k, v, lam_vecs, subln):
    b, t, d = q.shape
    hw = 2 * DIFF_HEAD
    tq = _row_tile(t, (344, 256, 128, 64, 8))
    return pl.pallas_call(
        _attn_prompt_kernel,
        grid=(b, d // hw, t // tq),
        in_specs=[
            _full(lam_vecs.shape),
            _full(subln.shape),
            pl.BlockSpec((1, tq, hw), lambda bi, h, qi: (bi, qi, h)),
            pl.BlockSpec((1, t, hw), lambda bi, h, qi: (bi, 0, h)),
            pl.BlockSpec((1, t, hw), lambda bi, h, qi: (bi, 0, h)),
        ],
        out_specs=pl.BlockSpec((1, tq, hw), lambda bi, h, qi: (bi, qi, h)),
        out_shape=jax.ShapeDtypeStruct((b, t, d), F32),
        compiler_params=_cparams("parallel", "parallel", "arbitrary"),
        name="attn_prompt",
    )(lam_vecs, subln, q, k, v)


def _attn_decode_kernel(pt_ref, lam_ref, subln_ref, q_ref, kc_ref, vc_ref, kn_ref, vn_ref, o_ref,
                        m_ref, l_ref, acc_ref):
    del pt_ref
    p = pl.program_id(1)
    n_pages = pl.num_programs(1) - 1
    tq = q_ref.shape[1]
    page = kc_ref.shape[1]
    hw = 2 * DIFF_HEAD
    n_heads = q_ref.shape[2] // hw
    rows = 2 * tq

    @pl.when(p == 0)
    def _():
        m_ref[...] = jnp.full(m_ref.shape, -jnp.inf, F32)
        l_ref[...] = jnp.zeros(l_ref.shape, F32)
        acc_ref[...] = jnp.zeros(acc_ref.shape, F32)

    def update(h, kh, vh, visible):
        qh = q_ref[0, :, h * hw:(h + 1) * hw] * (DIFF_HEAD ** -0.5)
        q0, q1 = _split_maps(qh)
        qq = jnp.concatenate([q0, q1], axis=0)
        s = _dot_nt(qq, kh)
        if visible is not None:
            s = jnp.where(visible, s, -jnp.inf)
        m = m_ref[h]
        m_new = jnp.maximum(m, jnp.max(s, axis=-1, keepdims=True))
        pr = jnp.exp(s - m_new)
        scale = jnp.exp(m - m_new)
        l_ref[h] = scale * l_ref[h] + jnp.sum(pr, axis=-1, keepdims=True)
        acc_ref[h] = scale * acc_ref[h] + _dot(pr.astype(BF16), vh)
        m_ref[h] = m_new

    @pl.when(p < n_pages)
    def _():
        for h in range(n_heads):
            update(h, kc_ref[0, :, h * hw:(h + 1) * hw].astype(BF16),
                   vc_ref[0, :, h * hw:(h + 1) * hw].astype(BF16), None)

    @pl.when(p == n_pages)
    def _():
        q_idx = lax.broadcasted_iota(jnp.int32, (rows, page), 0) % tq
        k_idx = lax.broadcasted_iota(jnp.int32, (rows, page), 1)
        visible = k_idx <= q_idx
        pad = jnp.zeros((page - tq, hw), BF16)
        lam = _lambda_value(lam_ref)
        for h in range(n_heads):
            kh = jnp.concatenate([kn_ref[0, :, h * hw:(h + 1) * hw].astype(BF16), pad], axis=0)
            vh = jnp.concatenate([vn_ref[0, :, h * hw:(h + 1) * hw].astype(BF16), pad], axis=0)
            update(h, kh, vh, visible)
            on = acc_ref[h] / l_ref[h]
            o = on[0:tq, :] - lam * on[tq:rows, :]
            o_ref[0, :, h * hw:(h + 1) * hw] = _sub_ln(o, subln_ref[...])


def _attn_decode(q, cache_k, cache_v, page_table, k_new, v_new, lam_vecs, subln):
    db, tq, d = q.shape
    n_pages = page_table.shape[1]
    page = cache_k.shape[1]
    hw = 2 * DIFF_HEAD
    n_heads = d // hw
    assert tq % SUBLANES == 0 and tq <= page and page == hw

    def page_map(bi, p, pt):
        return (pt[bi * n_pages + jnp.minimum(p, n_pages - 1)], 0, 0)

    def batch_map(bi, p, pt):
        return (bi, 0, 0)

    return pl.pallas_call(
        _attn_decode_kernel,
        grid_spec=pltpu.PrefetchScalarGridSpec(
            num_scalar_prefetch=1,
            grid=(db, n_pages + 1),
            in_specs=[
                pl.BlockSpec(lam_vecs.shape, lambda bi, p, pt: (0, 0)),
                pl.BlockSpec(subln.shape, lambda bi, p, pt: (0, 0)),
                pl.BlockSpec((1, tq, d), batch_map),
                pl.BlockSpec((1, page, d), page_map),
                pl.BlockSpec((1, page, d), page_map),
                pl.BlockSpec((1, tq, d), batch_map),
                pl.BlockSpec((1, tq, d), batch_map),
            ],
            out_specs=pl.BlockSpec((1, tq, d), batch_map),
            scratch_shapes=[
                pltpu.VMEM((n_heads, 2 * tq, hw), F32),
                pltpu.VMEM((n_heads, 2 * tq, hw), F32),
                pltpu.VMEM((n_heads, 2 * tq, hw), F32),
            ],
        ),
        out_shape=jax.ShapeDtypeStruct((db, tq, d), F32),
        compiler_params=_cparams("parallel", "arbitrary"),
        name="attn_decode",
    )(page_table.reshape(-1), lam_vecs, subln, q, cache_k, cache_v, k_new, v_new)


def _to_scan_layout(t, b, seq, heads):
    t = t.reshape(b, seq, heads, RWKV_HEAD)
    return jnp.transpose(t, (1, 3, 0, 2)).reshape(seq, RWKV_HEAD, b * heads)


def _from_scan_layout(t, b, seq, heads):
    t = t.reshape(seq, RWKV_HEAD, b, heads)
    return jnp.transpose(t, (2, 0, 3, 1)).reshape(b * seq, heads * RWKV_HEAD)


def _run(x, wkv_in, shift_in, past, pr):
    b, t, d = x.shape
    n = b * t
    heads = d // RWKV_HEAD
    xf = x.reshape(n, d)

    x_prev = jnp.concatenate([shift_in[:, None, :], x[:, :-1]], axis=1).reshape(n, d)
    r, w, k, v, a, bb, g = _rwkv_pre(xf, x_prev, pr["rwkv"])
    s0 = jnp.transpose(wkv_in, (3, 2, 0, 1)).reshape(RWKV_HEAD, RWKV_HEAD, b * heads)
    o_t, s_fin = _rwkv_scan(*[_to_scan_layout(z, b, t, heads) for z in (r, w, k, v, a, bb)], s0)
    o = _from_scan_layout(o_t, b, t, heads)
    new_wkv = jnp.transpose(s_fin.reshape(RWKV_HEAD, RWKV_HEAD, b, heads), (2, 3, 1, 0))[:, None]
    new_shift = x[:, -1][:, None]
    x1 = _rwkv_post(o, r, k, v, g, xf, pr["rwkv"], pr["post_ln"][0][0])
    y2 = _moe(x1, pr["router_wt"], pr["router_b"], *pr["moe"][0])
    x2 = _combine_ln(y2, x1, pr["post_ln"][0][1])

    k_new, v_new, q = _proj(x2, [pr["w_k_shared"], pr["w_v_shared"], pr["diff_w_q"]], "kv_q_proj")
    k3, v3, q3 = (z.reshape(b, t, d) for z in (k_new, v_new, q))
    if past is None:
        att = _attn_prompt(q3, k3, v3, pr["diff_lambda"], pr["diff_subln"])
    else:
        cache_k, cache_v, page_table = past
        att = _attn_decode(q3, cache_k, cache_v, page_table, k3, v3, pr["diff_lambda"], pr["diff_subln"])
    x3 = _proj_ln(att.reshape(n, d), x2, pr["diff_w_out"], pr["post_ln"][1][0])
    y2 = _moe(x3, pr["router_wt"], pr["router_b"], *pr["moe"][1])
    x4 = _combine_ln(y2, x3, pr["post_ln"][1][1])

    n_diff = d // (2 * DIFF_HEAD)
    return (x4.reshape(b, t, d), k_new.reshape(b, t, 2 * n_diff, DIFF_HEAD),
            v_new.reshape(b, t, n_diff, 2 * DIFF_HEAD), new_wkv, new_shift)


def kernel(x_prompt, x_sample, cache_k, cache_v, page_table, state_wkv, state_shift, meta_tokens,
           rwkv_mix, rwkv_w_rkv, rwkv_decay_w0, rwkv_decay_w1, rwkv_decay_w2, rwkv_a_w0, rwkv_a_w1,
           rwkv_a_w2, rwkv_g_w1, rwkv_g_w2, rwkv_kk_ka, rwkv_r_k, rwkv_lnx, rwkv_w_out, w_kv_shared,
           diff_w_q, diff_lambda, diff_subln, diff_w_out, router_w, router_b, moe_w_gate, moe_w_up,
           moe_w_down, post_ln):
    d = x_prompt.shape[-1]
    assert rwkv_mix.shape[0] == N_A_LAYERS and moe_w_gate.shape[0] == DEPTH
    ind, ind_t = _head_indicator(d, RWKV_HEAD)
    qk_width = diff_w_q.shape[-1]
    pr = {
        "rwkv": {
            "mix": rwkv_mix[0], "w_r": rwkv_w_rkv[0, 0].astype(BF16), "w_k": rwkv_w_rkv[0, 1].astype(BF16),
            "w_v": rwkv_w_rkv[0, 2].astype(BF16), "dw0": rwkv_decay_w0, "dw1": rwkv_decay_w1[0].astype(BF16),
            "dw2": rwkv_decay_w2[0].astype(BF16), "aw0": rwkv_a_w0, "aw1": rwkv_a_w1[0].astype(BF16),
            "aw2": rwkv_a_w2[0].astype(BF16), "gw1": rwkv_g_w1[0].astype(BF16),
            "gw2": rwkv_g_w2[0].astype(BF16), "kk_ka": rwkv_kk_ka[0], "r_k": rwkv_r_k[0].reshape(1, d),
            "lnx": rwkv_lnx[0], "w_out": rwkv_w_out[0].astype(BF16), "ind": ind, "ind_t": ind_t,
        },
        "post_ln": post_ln,
        "router_wt": router_w.T,
        "router_b": router_b.reshape(N_EXPERTS, 1),
        "moe": [(moe_w_gate[l].astype(BF16), moe_w_up[l].astype(BF16), moe_w_down[l].astype(BF16))
                for l in range(DEPTH)],
        "w_k_shared": w_kv_shared[:, :qk_width].astype(BF16),
        "w_v_shared": w_kv_shared[:, qk_width:].astype(BF16),
        "diff_w_q": diff_w_q[0].astype(BF16),
        "diff_lambda": diff_lambda[0],
        "diff_subln": diff_subln[0].reshape(1, -1),
        "diff_w_out": diff_w_out[0].astype(BF16),
    }

    bp = x_prompt.shape[0]
    n_meta = meta_tokens.shape[0]
    meta = jnp.broadcast_to(meta_tokens[None].astype(x_prompt.dtype), (bp, n_meta, d))
    xp = jnp.concatenate([meta, x_prompt], axis=1)
    wkv0 = jnp.zeros((bp,) + state_wkv.shape[2:], x_prompt.dtype)
    shift0 = jnp.zeros((bp, d), x_prompt.dtype)
    yp, k_p, v_p, wkv_p, shift_p = _run(xp, wkv0, shift0, None, pr)

    n_pool, page = cache_k.shape[:2]
    past = (cache_k.reshape(n_pool, page, d), cache_v.reshape(n_pool, page, d), page_table)
    ys, k_s, v_s, wkv_s, shift_s = _run(x_sample, state_wkv[:, 0], state_shift[:, 0], past, pr)
    return (yp[:, n_meta:], ys, k_p, v_p, wkv_p, shift_p, k_s, v_s, wkv_s, shift_s)
```

```python
import functools
import math

import jax
import jax.numpy as jnp
from jax import lax
from jax.experimental import pallas as pl
from jax.experimental.pallas import tpu as pltpu

F32 = jnp.float32
BF16 = jnp.bfloat16

DEPTH = 2
N_A_LAYERS = 1
RWKV_HEAD = 64
DIFF_HEAD = 64
LNX_EPS = 64e-5
SUBLN_EPS = 1e-5
LN_EPS = 1e-5
N_EXPERTS = 16
N_GROUPS = 4
EXPERTS_PER_GROUP = N_EXPERTS // N_GROUPS
TOP_K = 2
DEEPNORM_ALPHA = (2 * DEPTH) ** 0.25
LAMBDA_INIT = 0.8 - 0.6 * math.exp(-0.3 * 1)

LANES = 128
SUBLANES = 8
VMEM_LIMIT_BYTES = 56 * 1024 * 1024


def _cparams(*sem):
    return pltpu.CompilerParams(dimension_semantics=sem, vmem_limit_bytes=VMEM_LIMIT_BYTES)


def _row_tile(n, candidates):
    for c in candidates:
        if n % c == 0:
            return c
    raise ValueError(f"no row tile for {n} in {candidates}")


def _full(shape):
    zeros = (0,) * len(shape)
    return pl.BlockSpec(shape, lambda *_: zeros)


def _dot(a, b):
    return jnp.dot(a, b, preferred_element_type=F32)


def _dot_nt(a, b):
    return lax.dot_general(a, b, (((1,), (1,)), ((), ())), preferred_element_type=F32)


def _split_dot(x, w_bf16):
    hi = x.astype(BF16)
    lo = (x - hi.astype(F32)).astype(BF16)
    return _dot(hi, w_bf16) + _dot(lo, w_bf16)


def _layernorm(z, g, b):
    mu = jnp.mean(z, axis=-1, keepdims=True)
    zc = z - mu
    var = jnp.mean(zc * zc, axis=-1, keepdims=True)
    return zc * lax.rsqrt(var + LN_EPS) * g + b


def _head_indicator(d, head):
    n_heads = d // head
    assert n_heads <= LANES
    ind = (jnp.arange(d)[:, None] // head == jnp.arange(LANES)[None, :]).astype(BF16)
    return ind, ind.T


def _rwkv_pre_kernel(x_ref, xp_ref, mix_ref, wr_ref, wk_ref, wv_ref, dw0_ref, dw1_ref, dw2_ref,
                     aw0_ref, aw1_ref, aw2_ref, gw1_ref, gw2_ref, kkka_ref, ind_ref, indt_ref,
                     r_ref, w_ref, k_ref, v_ref, a_ref, b_ref, g_ref):
    x = x_ref[...]
    xx = xp_ref[...] - x

    def mixed(i):
        return (x + xx * mix_ref[i:i + 1, :]).astype(BF16)

    r = _dot(mixed(0), wr_ref[...])
    k = _dot(mixed(2), wk_ref[...])
    v = _dot(mixed(3), wv_ref[...])
    lw = jnp.tanh(_dot(mixed(1), dw1_ref[...])).astype(BF16)
    z = -(dw0_ref[...] + _dot(lw, dw2_ref[...]))
    softplus = jnp.maximum(z, 0.0) + jnp.log(1.0 + jnp.exp(-jnp.abs(z)))
    w_log = -softplus - 0.5
    decay = jnp.exp(-jnp.exp(w_log))
    la = _dot(mixed(4), aw1_ref[...]).astype(BF16)
    a = jax.nn.sigmoid(aw0_ref[...] + _dot(la, aw2_ref[...]))
    lg = jax.nn.sigmoid(_dot(mixed(5), gw1_ref[...])).astype(BF16)
    g = _dot(lg, gw2_ref[...])

    kk = k * kkka_ref[0:1, :]
    ss = _split_dot(_split_dot(kk * kk, ind_ref[...]), indt_ref[...])
    kk = kk / jnp.maximum(jnp.sqrt(ss), 1e-12)
    k = k * (1.0 + (a - 1.0) * kkka_ref[1:2, :])

    r_ref[...] = r
    w_ref[...] = decay
    k_ref[...] = k
    v_ref[...] = v
    a_ref[...] = -kk
    b_ref[...] = kk * a
    g_ref[...] = g


def _rwkv_pre(x, x_prev, p):
    n, d = x.shape
    tm = _row_tile(n, (192, 128, 64, 8))
    row = pl.BlockSpec((tm, d), lambda i: (i, 0))
    weights = [p["mix"], p["w_r"], p["w_k"], p["w_v"], p["dw0"], p["dw1"], p["dw2"], p["aw0"],
               p["aw1"], p["aw2"], p["gw1"], p["gw2"], p["kk_ka"], p["ind"], p["ind_t"]]
    return pl.pallas_call(
        _rwkv_pre_kernel,
        grid=(n // tm,),
        in_specs=[row, row] + [_full(w.shape) for w in weights],
        out_specs=[row] * 7,
        out_shape=[jax.ShapeDtypeStruct((n, d), F32)] * 7,
        compiler_params=_cparams("parallel"),
        name="rwkv_pre",
    )(x, x_prev, *weights)


def _rwkv_scan_kernel(r_ref, w_ref, k_ref, v_ref, a_ref, b_ref, s0_ref, o_ref, sout_ref, s_ref):
    tc = pl.program_id(1)
    n_tc = pl.num_programs(1)
    steps, nk, _ = r_ref.shape

    @pl.when(tc == 0)
    def _():
        s_ref[...] = s0_ref[...]

    def step(t, carry):
        vt = v_ref[t]
        sa = jnp.zeros_like(vt)
        for j in range(nk):
            sa = sa + s_ref[j] * a_ref[t, pl.ds(j, 1), :]
        ot = jnp.zeros_like(vt)
        for j in range(nk):
            s = (s_ref[j] * w_ref[t, pl.ds(j, 1), :] + sa * b_ref[t, pl.ds(j, 1), :]
                 + vt * k_ref[t, pl.ds(j, 1), :])
            s_ref[j] = s
            ot = ot + s * r_ref[t, pl.ds(j, 1), :]
        o_ref[t] = ot
        return carry

    lax.fori_loop(0, steps, step, 0)

    @pl.when(tc == n_tc - 1)
    def _():
        sout_ref[...] = s_ref[...]


def _rwkv_scan(r, w, k, v, a, b, s0):
    t, nk, l = r.shape
    steps = _row_tile(t, (16, 8, 4, 2, 1))
    seq = pl.BlockSpec((steps, nk, LANES), lambda g, c: (c, 0, g))
    st = pl.BlockSpec((nk, nk, LANES), lambda g, c: (0, 0, g))
    return pl.pallas_call(
        _rwkv_scan_kernel,
        grid=(l // LANES, t // steps),
        in_specs=[seq] * 6 + [st],
        out_specs=[seq, st],
        out_shape=[jax.ShapeDtypeStruct((t, nk, l), F32), jax.ShapeDtypeStruct((nk, nk, l), F32)],
        scratch_shapes=[pltpu.VMEM((nk, nk, LANES), F32)],
        compiler_params=_cparams("parallel", "arbitrary"),
        name="rwkv_scan",
    )(r, w, k, v, a, b, s0)


def _rwkv_post_kernel(o_ref, r_ref, k_ref, v_ref, g_ref, x_ref, lnx_ref, rk_ref, ind_ref, indt_ref,
                      wout_ref, ln_ref, out_ref):
    ind = ind_ref[...]
    ind_t = indt_ref[...]
    inv_head = 1.0 / RWKV_HEAD
    o = o_ref[...]
    mu = _split_dot(_split_dot(o, ind) * inv_head, ind_t)
    oc = o - mu
    var = _split_dot(oc * oc, ind) * inv_head
    rstd = _split_dot(lax.rsqrt(var + LNX_EPS), ind_t)
    on = oc * rstd * lnx_ref[0:1, :] + lnx_ref[1:2, :]
    v = v_ref[...]
    bonus = _split_dot(_split_dot(r_ref[...] * k_ref[...] * rk_ref[...], ind), ind_t)
    on = on + bonus * v
    h = _dot((on * g_ref[...]).astype(BF16), wout_ref[...])
    out_ref[...] = _layernorm(DEEPNORM_ALPHA * x_ref[...] + h, ln_ref[0:1, :], ln_ref[1:2, :])


def _rwkv_post(o, r, k, v, g, x, p, ln):
    n, d = x.shape
    tm = _row_tile(n, (384, 256, 128, 64, 8))
    row = pl.BlockSpec((tm, d), lambda i: (i, 0))
    weights = [p["lnx"], p["r_k"], p["ind"], p["ind_t"], p["w_out"], ln]
    return pl.pallas_call(
        _rwkv_post_kernel,
        grid=(n // tm,),
        in_specs=[row] * 6 + [_full(w.shape) for w in weights],
        out_specs=row,
        out_shape=jax.ShapeDtypeStruct((n, d), F32),
        compiler_params=_cparams("parallel"),
        name="rwkv_post",
    )(o, r, k, v, g, x, *weights)


def _proj_kernel(*refs, n_out):
    x_ref = refs[0]
    w_refs = refs[1:1 + n_out]
    out_refs = refs[1 + n_out:]
    x = x_ref[...].astype(BF16)
    for w_ref, out_ref in zip(w_refs, out_refs):
        out_ref[...] = _dot(x, w_ref[...])


def _proj(x, ws, name):
    n, d = x.shape
    tm = _row_tile(n, (384, 256, 128, 64, 8))
    return pl.pallas_call(
        functools.partial(_proj_kernel, n_out=len(ws)),
        grid=(n // tm,),
        in_specs=[pl.BlockSpec((tm, d), lambda i: (i, 0))] + [_full(w.shape) for w in ws],
        out_specs=[pl.BlockSpec((tm, w.shape[1]), lambda i: (i, 0)) for w in ws],
        out_shape=[jax.ShapeDtypeStruct((n, w.shape[1]), F32) for w in ws],
        compiler_params=_cparams("parallel"),
        name=name,
    )(x, *ws)


def _proj_ln_kernel(a_ref, x_ref, w_ref, ln_ref, out_ref):
    h = _dot(a_ref[...].astype(BF16), w_ref[...])
    out_ref[...] = _layernorm(DEEPNORM_ALPHA * x_ref[...] + h, ln_ref[0:1, :], ln_ref[1:2, :])


def _proj_ln(a, x, w, ln):
    n, d = x.shape
    tm = _row_tile(n, (384, 256, 128, 64, 8))
    row = pl.BlockSpec((tm, d), lambda i: (i, 0))
    return pl.pallas_call(
        _proj_ln_kernel,
        grid=(n // tm,),
        in_specs=[pl.BlockSpec((tm, a.shape[1]), lambda i: (i, 0)), row, _full(w.shape), _full(ln.shape)],
        out_specs=row,
        out_shape=jax.ShapeDtypeStruct((n, d), F32),
        compiler_params=_cparams("parallel"),
        name="proj_ln",
    )(a, x, w, ln)


MOE_TILE = 256


def _router_kernel(x_ref, wt_ref, bias_ref, tri_ref, eidx_ref, wts_ref, rank_ref, cnt_ref, run_ref):
    step = pl.program_id(0)

    @pl.when(step == 0)
    def _():
        run_ref[...] = jnp.zeros(run_ref.shape, F32)

    x = x_ref[...]
    x_hi = x.astype(BF16)
    x_lo = (x - x_hi.astype(F32)).astype(BF16)
    wt = wt_ref[...]
    wt_hi = wt.astype(BF16)
    wt_lo = (wt - wt_hi.astype(F32)).astype(BF16)
    logits = _dot_nt(wt_hi, x_hi) + _dot_nt(wt_hi, x_lo) + _dot_nt(wt_lo, x_hi)
    s = jax.nn.sigmoid(logits)
    sel = s + bias_ref[...]
    s_rows = [s[e:e + 1, :] for e in range(N_EXPERTS)]
    sel_rows = [sel[e:e + 1, :] for e in range(N_EXPERTS)]

    best_grp = None
    best_idx = None
    for gi in range(N_GROUPS):
        rows = sel_rows[gi * EXPERTS_PER_GROUP:(gi + 1) * EXPERTS_PER_GROUP]
        score = None
        for i in range(EXPERTS_PER_GROUP):
            for j in range(i + 1, EXPERTS_PER_GROUP):
                pair = rows[i] + rows[j]
                score = pair if score is None else jnp.maximum(score, pair)
        if best_grp is None:
            best_grp = score
            best_idx = jnp.zeros(score.shape, jnp.int32)
        else:
            better = score > best_grp
            best_grp = jnp.where(better, score, best_grp)
            best_idx = jnp.where(better, gi, best_idx)

    neg_inf = jnp.float32(-jnp.inf)
    masked = [jnp.where(best_idx == e // EXPERTS_PER_GROUP, sel_rows[e], neg_inf)
              for e in range(N_EXPERTS)]

    def argmax_rows(exclude):
        top = jnp.full(masked[0].shape, neg_inf)
        idx = jnp.full(masked[0].shape, -1, jnp.int32)
        val = jnp.zeros(masked[0].shape, F32)
        for e in range(N_EXPERTS):
            cand = masked[e] if exclude is None else jnp.where(exclude == e, neg_inf, masked[e])
            better = cand > top
            top = jnp.where(better, cand, top)
            idx = jnp.where(better, e, idx)
            val = jnp.where(better, s_rows[e], val)
        return idx, val

    i1, w1 = argmax_rows(None)
    i2, w2 = argmax_rows(i1)
    tot = w1 + w2
    eidx_ref[0:1, :] = i1
    eidx_ref[1:2, :] = i2
    wts_ref[0:1, :] = w1 / tot
    wts_ref[1:2, :] = w2 / tot

    e_iota = lax.broadcasted_iota(jnp.int32, logits.shape, 0)
    run = run_ref[...]
    for slot, idx in enumerate((i1, i2)):
        onehot = jnp.where(e_iota == idx, 1.0, 0.0)
        prefix = _dot(onehot.astype(BF16), tri_ref[...])
        rank = jnp.sum(onehot * (run[:, 0:1] + prefix - 1.0), axis=0, keepdims=True)
        rank_ref[slot:slot + 1, :] = rank.astype(jnp.int32)
        run = run + jnp.sum(onehot, axis=-1, keepdims=True)
    run_ref[...] = run
    cnt_ref[...] = run


def _router(x, router_wt, router_b):
    n, d = x.shape
    tm = _row_tile(n, (384, 256, 128))
    tri = (jnp.arange(tm)[:, None] <= jnp.arange(tm)[None, :]).astype(BF16)
    pair = pl.BlockSpec((TOP_K, tm), lambda i: (0, i))
    return pl.pallas_call(
        _router_kernel,
        grid=(n // tm,),
        in_specs=[pl.BlockSpec((tm, d), lambda i: (i, 0)), _full(router_wt.shape), _full(router_b.shape),
                  _full(tri.shape)],
        out_specs=[pair, pair, pair, _full((N_EXPERTS, LANES))],
        out_shape=[jax.ShapeDtypeStruct((TOP_K, n), jnp.int32), jax.ShapeDtypeStruct((TOP_K, n), F32),
                   jax.ShapeDtypeStruct((TOP_K, n), jnp.int32), jax.ShapeDtypeStruct((N_EXPERTS, LANES), F32)],
        scratch_shapes=[pltpu.VMEM((N_EXPERTS, LANES), F32)],
        compiler_params=_cparams("arbitrary"),
        name="router",
    )(x, router_wt, router_b, tri)


DMA_UNROLL = 8


def _dispatch_kernel(pos_ref, x_ref, init_hbm, xs_hbm, sem):
    del init_hbm
    tm = x_ref.shape[0]

    def row_copy(slot, r):
        return pltpu.make_async_copy(x_ref.at[pl.ds(r, 1)], xs_hbm.at[pl.ds(pos_ref[0, slot, r], 1)], sem)

    def start(r, c):
        for slot in range(TOP_K):
            row_copy(slot, r).start()
        return c

    def wait(r, c):
        for slot in range(TOP_K):
            row_copy(slot, r).wait()
        return c

    lax.fori_loop(0, tm, start, 0, unroll=DMA_UNROLL)
    lax.fori_loop(0, tm, wait, 0, unroll=DMA_UNROLL)


def _dispatch(x, pos_blocks, n_rows):
    n, d = x.shape
    nt, _, tm = pos_blocks.shape
    return pl.pallas_call(
        _dispatch_kernel,
        grid=(nt,),
        in_specs=[pl.BlockSpec((1, TOP_K, tm), lambda i: (i, 0, 0), memory_space=pltpu.SMEM),
                  pl.BlockSpec((tm, d), lambda i: (i, 0)),
                  pl.BlockSpec(memory_space=pl.ANY)],
        out_specs=pl.BlockSpec(memory_space=pl.ANY),
        out_shape=jax.ShapeDtypeStruct((n_rows, d), F32),
        scratch_shapes=[pltpu.SemaphoreType.DMA(())],
        input_output_aliases={2: 0},
        compiler_params=_cparams("arbitrary"),
        name="moe_dispatch",
    )(pos_blocks, x, jnp.zeros((n_rows, d), F32))


def _experts_kernel(tile_e_ref, n_used_ref, xs_ref, wg_ref, wu_ref, wd_ref, ys_ref):
    del tile_e_ref
    i = pl.program_id(0)

    @pl.when(i < n_used_ref[0])
    def _():
        x = xs_ref[...].astype(BF16)
        hg = _dot(x, wg_ref[0])
        hu = _dot(x, wu_ref[0])
        h = (hg * jax.nn.sigmoid(hg) * hu).astype(BF16)
        ys_ref[...] = _dot(h, wd_ref[0])

    @pl.when(i >= n_used_ref[0])
    def _():
        ys_ref[...] = jnp.zeros(ys_ref.shape, F32)


def _experts(xs, tile_e, n_used, wg, wu, wd):
    n_rows, d = xs.shape
    d_exp = wg.shape[2]
    tm = MOE_TILE

    def w_map(i, te, nu):
        return (te[i], 0, 0)

    return pl.pallas_call(
        _experts_kernel,
        grid_spec=pltpu.PrefetchScalarGridSpec(
            num_scalar_prefetch=2,
            grid=(n_rows // tm,),
            in_specs=[
                pl.BlockSpec((tm, d), lambda i, te, nu: (jnp.minimum(i, nu[0] - 1), 0)),
                pl.BlockSpec((1, d, d_exp), w_map),
                pl.BlockSpec((1, d, d_exp), w_map),
                pl.BlockSpec((1, d_exp, d), w_map),
            ],
            out_specs=pl.BlockSpec((tm, d), lambda i, te, nu: (i, 0)),
        ),
        out_shape=jax.ShapeDtypeStruct((n_rows, d), F32),
        compiler_params=_cparams("arbitrary"),
        name="moe_experts",
    )(tile_e, n_used, xs, wg, wu, wd)


def _combine_ln_kernel(pos_ref, wts_ref, x_ref, ln_ref, ys_hbm, out_ref, ybuf, sem):
    tm = x_ref.shape[0]

    def row_copy(slot, r):
        return pltpu.make_async_copy(ys_hbm.at[pl.ds(pos_ref[0, slot, r], 1)], ybuf.at[slot, pl.ds(r, 1)], sem)

    def start(r, c):
        for slot in range(TOP_K):
            row_copy(slot, r).start()
        return c

    def wait(r, c):
        for slot in range(TOP_K):
            row_copy(slot, r).wait()
        return c

    lax.fori_loop(0, tm, start, 0, unroll=DMA_UNROLL)
    lax.fori_loop(0, tm, wait, 0, unroll=DMA_UNROLL)
    y = wts_ref[:, 0:1] * ybuf[0] + wts_ref[:, 1:2] * ybuf[1]
    out_ref[...] = _layernorm(DEEPNORM_ALPHA * x_ref[...] + y, ln_ref[0:1, :], ln_ref[1:2, :])


def _combine_ln(ys, pos_blocks, wts_t, x, ln):
    n, d = x.shape
    nt, _, tm = pos_blocks.shape
    row = pl.BlockSpec((tm, d), lambda i: (i, 0))
    return pl.pallas_call(
        _combine_ln_kernel,
        grid=(nt,),
        in_specs=[pl.BlockSpec((1, TOP_K, tm), lambda i: (i, 0, 0), memory_space=pltpu.SMEM),
                  pl.BlockSpec((tm, TOP_K), lambda i: (i, 0)), row, _full(ln.shape),
                  pl.BlockSpec(memory_space=pl.ANY)],
        out_specs=row,
        out_shape=jax.ShapeDtypeStruct((n, d), F32),
        scratch_shapes=[pltpu.VMEM((TOP_K, tm, d), F32), pltpu.SemaphoreType.DMA(())],
        compiler_params=_cparams("arbitrary"),
        name="moe_combine_ln",
    )(pos_blocks, wts_t, x, ln, ys)


def _moe_ln(x, ln, router_wt, router_b, wg, wu, wd):
    n, d = x.shape
    eidx, wts, rank, cnt = _router(x, router_wt, router_b)

    n_tiles = pl.cdiv(TOP_K * n, MOE_TILE) + N_EXPERTS
    counts = cnt[:, 0].astype(jnp.int32)
    tiles_per_e = (counts + MOE_TILE - 1) // MOE_TILE
    tile_end = jnp.cumsum(tiles_per_e)
    tile_start = tile_end - tiles_per_e
    n_used = tile_end[-1:]
    experts = jnp.arange(N_EXPERTS, dtype=jnp.int32)
    base = jnp.sum(jnp.where(eidx[..., None] == experts, tile_start * MOE_TILE, 0), axis=-1)
    pos = base + rank
    tile_ids = jnp.minimum(jnp.arange(n_tiles, dtype=jnp.int32), n_used - 1)
    tile_e = jnp.sum((tile_end[None, :] <= tile_ids[:, None]).astype(jnp.int32), axis=1)
    tile_e = jnp.minimum(tile_e, N_EXPERTS - 1)

    tm = _row_tile(n, (384, 256, 128))
    pos_blocks = jnp.transpose(pos.reshape(TOP_K, n // tm, tm), (1, 0, 2))
    xs = _dispatch(x, pos_blocks, n_tiles * MOE_TILE)
    ys = _experts(xs, tile_e, n_used, wg, wu, wd)
    return _combine_ln(ys, pos_blocks, wts.T, x, ln)


def _lambda_value(lam_ref):
    lf = lam_ref[...]
    s01 = jnp.sum(lf[0:1, :] * lf[1:2, :], axis=-1, keepdims=True)
    s23 = jnp.sum(lf[2:3, :] * lf[3:4, :], axis=-1, keepdims=True)
    return jnp.exp(s01) - jnp.exp(s23) + LAMBDA_INIT


def _sub_ln(o, subln):
    o = o * lax.rsqrt(jnp.mean(o * o, axis=-1, keepdims=True) + SUBLN_EPS) * subln
    return o * (1.0 - LAMBDA_INIT)


def _split_maps(q):
    lane = lax.broadcasted_iota(jnp.int32, q.shape, 1)
    q0 = jnp.where(lane < DIFF_HEAD, q, 0.0).astype(BF16)
    q1 = jnp.where(lane >= DIFF_HEAD, q, 0.0).astype(BF16)
    return q0, q1


def _attn_prompt_kernel(lam_ref, subln_ref, q_ref, k_ref, v_ref, o_ref):
    qi = pl.program_id(2)
    tq = q_ref.shape[1]
    q0, q1 = _split_maps(q_ref[0] * (DIFF_HEAD ** -0.5))

    def chunk(j, carry, visible):
        start = pl.multiple_of(j * tq, SUBLANES)
        kc = k_ref[0, pl.ds(start, tq), :].astype(BF16)
        vc = v_ref[0, pl.ds(start, tq), :].astype(BF16)
        out = []
        for qm, (m, l, acc) in zip((q0, q1), carry):
            s = _dot_nt(qm, kc)
            if visible is not None:
                s = jnp.where(visible, s, -jnp.inf)
            m_new = jnp.maximum(m, jnp.max(s, axis=-1, keepdims=True))
            p = jnp.exp(s - m_new)
            scale = jnp.exp(m - m_new)
            l_new = scale * l + jnp.sum(p, axis=-1, keepdims=True)
            acc_new = scale * acc + _dot(p.astype(BF16), vc)
            out.append((m_new, l_new, acc_new))
        return tuple(out)

    init = (jnp.full((tq, 1), -jnp.inf, F32), jnp.zeros((tq, 1), F32), jnp.zeros((tq, 2 * DIFF_HEAD), F32))
    carry = lax.fori_loop(0, qi, functools.partial(chunk, visible=None), (init, init))
    causal = (lax.broadcasted_iota(jnp.int32, (tq, tq), 1) <= lax.broadcasted_iota(jnp.int32, (tq, tq), 0))
    (_, l0, a0), (_, l1, a1) = chunk(qi, carry, causal)
    o = a0 / l0 - _lambda_value(lam_ref) * (a1 / l1)
    o_ref[0] = _sub_ln(o, subln_ref[...])


def _attn_prompt(q, k, v, lam_vecs, subln):
    b, t, d = q.shape
    hw = 2 * DIFF_HEAD
    tq = _row_tile(t, (344, 256, 128, 64, 8))
    return pl.pallas_call(
        _attn_prompt_kernel,
        grid=(b, d // hw, t // tq),
        in_specs=[
            _full(lam_vecs.shape),
            _full(subln.shape),
            pl.BlockSpec((1, tq, hw), lambda bi, h, qi: (bi, qi, h)),
            pl.BlockSpec((1, t, hw), lambda bi, h, qi: (bi, 0, h)),
            pl.BlockSpec((1, t, hw), lambda bi, h, qi: (bi, 0, h)),
        ],
        out_specs=pl.BlockSpec((1, tq, hw), lambda bi, h, qi: (bi, qi, h)),
        out_shape=jax.ShapeDtypeStruct((b, t, d), F32),
        compiler_params=_cparams("parallel", "parallel", "arbitrary"),
        name="attn_prompt",
    )(lam_vecs, subln, q, k, v)


PAGES_PER_STEP = 4


def _attn_decode_kernel(pt_ref, lam_ref, subln_ref, q_ref, *refs):
    del pt_ref
    g = PAGES_PER_STEP
    kc_refs, vc_refs = refs[0:g], refs[g:2 * g]
    kn_ref, vn_ref, o_ref, qq_ref, m_ref, l_ref, acc_ref = refs[2 * g:]
    p = pl.program_id(1)
    tq = q_ref.shape[1]
    page = kc_refs[0].shape[1]
    hw = 2 * DIFF_HEAD
    n_heads = q_ref.shape[2] // hw
    rows = 2 * tq

    @pl.when(p == 0)
    def _():
        m_ref[...] = jnp.full(m_ref.shape, -jnp.inf, F32)
        l_ref[...] = jnp.zeros(l_ref.shape, F32)
        acc_ref[...] = jnp.zeros(acc_ref.shape, F32)
        for h in range(n_heads):
            q0, q1 = _split_maps(q_ref[0, :, h * hw:(h + 1) * hw] * (DIFF_HEAD ** -0.5))
            qq_ref[h] = jnp.concatenate([q0, q1], axis=0)

    def update(k_blocks, v_blocks, visible):
        nb = len(k_blocks)
        s = [jnp.concatenate([_dot_nt(qq_ref[h], k_blocks[j][h]) for h in range(n_heads)], axis=0)
             for j in range(nb)]
        if visible is not None:
            s = [jnp.where(visible, sj, -jnp.inf) for sj in s]
        m_old = m_ref[...]
        m_new = jnp.maximum(m_old, jnp.max(functools.reduce(jnp.maximum, s), axis=-1, keepdims=True))
        pr = [jnp.exp(sj - m_new) for sj in s]
        scale = jnp.exp(m_old - m_new)
        l_ref[...] = scale * l_ref[...] + jnp.sum(functools.reduce(jnp.add, pr), axis=-1, keepdims=True)
        pr = [pj.astype(BF16) for pj in pr]
        pv = []
        for h in range(n_heads):
            acc_h = None
            for j in range(nb):
                d = _dot(pr[j][h * rows:(h + 1) * rows, :], v_blocks[j][h])
                acc_h = d if acc_h is None else acc_h + d
            pv.append(acc_h)
        acc_ref[...] = scale * acc_ref[...] + jnp.concatenate(pv, axis=0)
        m_ref[...] = m_new

    def head_slices(ref):
        return [ref[0, :, h * hw:(h + 1) * hw].astype(BF16) for h in range(n_heads)]

    update([head_slices(r) for r in kc_refs], [head_slices(r) for r in vc_refs], None)

    @pl.when(p == pl.num_programs(1) - 1)
    def _():
        q_idx = lax.broadcasted_iota(jnp.int32, (n_heads * rows, page), 0) % tq
        k_idx = lax.broadcasted_iota(jnp.int32, (n_heads * rows, page), 1)
        pad = jnp.zeros((page - tq, hw), BF16)
        update([[jnp.concatenate([kh, pad], axis=0) for kh in head_slices(kn_ref)]],
               [[jnp.concatenate([vh, pad], axis=0) for vh in head_slices(vn_ref)]], k_idx <= q_idx)
        lam = _lambda_value(lam_ref)
        on = acc_ref[...] / l_ref[...]
        for h in range(n_heads):
            o = on[h * rows:h * rows + tq, :] - lam * on[h * rows + tq:(h + 1) * rows, :]
            o_ref[0, :, h * hw:(h + 1) * hw] = _sub_ln(o, subln_ref[...])


def _attn_decode(q, cache_k, cache_v, page_table, k_new, v_new, lam_vecs, subln):
    db, tq, d = q.shape
    n_pages = page_table.shape[1]
    page = cache_k.shape[1]
    hw = 2 * DIFF_HEAD
    n_heads = d // hw
    g = PAGES_PER_STEP
    assert tq % SUBLANES == 0 and tq <= page and page == hw and n_pages % g == 0

    def page_map(j):
        return lambda bi, p, pt: (pt[bi * n_pages + p * g + j], 0, 0)

    def batch_map(bi, p, pt):
        return (bi, 0, 0)

    page_specs = [pl.BlockSpec((1, page, d), page_map(j)) for j in range(g)]
    stat = pltpu.VMEM((n_heads * 2 * tq, hw), F32)
    return pl.pallas_call(
        _attn_decode_kernel,
        grid_spec=pltpu.PrefetchScalarGridSpec(
            num_scalar_prefetch=1,
            grid=(db, n_pages // g),
            in_specs=[
                pl.BlockSpec(lam_vecs.shape, lambda bi, p, pt: (0, 0)),
                pl.BlockSpec(subln.shape, lambda bi, p, pt: (0, 0)),
                pl.BlockSpec((1, tq, d), batch_map),
                *page_specs,
                *page_specs,
                pl.BlockSpec((1, tq, d), batch_map),
                pl.BlockSpec((1, tq, d), batch_map),
            ],
            out_specs=pl.BlockSpec((1, tq, d), batch_map),
            scratch_shapes=[pltpu.VMEM((n_heads, 2 * tq, hw), BF16), stat, stat, stat],
        ),
        out_shape=jax.ShapeDtypeStruct((db, tq, d), F32),
        compiler_params=_cparams("parallel", "arbitrary"),
        name="attn_decode",
    )(page_table.reshape(-1), lam_vecs, subln, q, *([cache_k] * g), *([cache_v] * g), k_new, v_new)


def _to_scan_layout(t, b, seq, heads):
    t = t.reshape(b, seq, heads, RWKV_HEAD)
    return jnp.transpose(t, (1, 3, 2, 0)).reshape(seq, RWKV_HEAD, heads * b)


def _from_scan_layout(t, b, seq, heads):
    t = t.reshape(seq, RWKV_HEAD, heads, b)
    return jnp.transpose(t, (3, 0, 2, 1)).reshape(b * seq, heads * RWKV_HEAD)


def _run(x, wkv_in, shift_in, past, pr):
    b, t, d = x.shape
    n = b * t
    heads = d // RWKV_HEAD
    xf = x.reshape(n, d)

    x_prev = jnp.concatenate([shift_in[:, None, :], x[:, :-1]], axis=1).reshape(n, d)
    r, w, k, v, a, bb, g = _rwkv_pre(xf, x_prev, pr["rwkv"])
    s0 = jnp.transpose(wkv_in, (3, 2, 1, 0)).reshape(RWKV_HEAD, RWKV_HEAD, heads * b)
    o_t, s_fin = _rwkv_scan(*[_to_scan_layout(z, b, t, heads) for z in (r, w, k, v, a, bb)], s0)
    o = _from_scan_layout(o_t, b, t, heads)
    new_wkv = jnp.transpose(s_fin.reshape(RWKV_HEAD, RWKV_HEAD, heads, b), (3, 2, 1, 0))[:, None]
    new_shift = x[:, -1][:, None]
    x1 = _rwkv_post(o, r, k, v, g, xf, pr["rwkv"], pr["post_ln"][0][0])
    x2 = _moe_ln(x1, pr["post_ln"][0][1], pr["router_wt"], pr["router_b"], *pr["moe"][0])

    k_new, v_new, q = _proj(x2, [pr["w_k_shared"], pr["w_v_shared"], pr["diff_w_q"]], "kv_q_proj")
    k3, v3, q3 = (z.reshape(b, t, d) for z in (k_new, v_new, q))
    if past is None:
        att = _attn_prompt(q3, k3, v3, pr["diff_lambda"], pr["diff_subln"])
    else:
        cache_k, cache_v, page_table = past
        att = _attn_decode(q3, cache_k, cache_v, page_table, k3, v3, pr["diff_lambda"], pr["diff_subln"])
    x3 = _proj_ln(att.reshape(n, d), x2, pr["diff_w_out"], pr["post_ln"][1][0])
    x4 = _moe_ln(x3, pr["post_ln"][1][1], pr["router_wt"], pr["router_b"], *pr["moe"][1])

    n_diff = d // (2 * DIFF_HEAD)
    return (x4.reshape(b, t, d), k_new.reshape(b, t, 2 * n_diff, DIFF_HEAD),
            v_new.reshape(b, t, n_diff, 2 * DIFF_HEAD), new_wkv, new_shift)


def kernel(x_prompt, x_sample, cache_k, cache_v, page_table, state_wkv, state_shift, meta_tokens,
           rwkv_mix, rwkv_w_rkv, rwkv_decay_w0, rwkv_decay_w1, rwkv_decay_w2, rwkv_a_w0, rwkv_a_w1,
           rwkv_a_w2, rwkv_g_w1, rwkv_g_w2, rwkv_kk_ka, rwkv_r_k, rwkv_lnx, rwkv_w_out, w_kv_shared,
           diff_w_q, diff_lambda, diff_subln, diff_w_out, router_w, router_b, moe_w_gate, moe_w_up,
           moe_w_down, post_ln):
    d = x_prompt.shape[-1]
    assert rwkv_mix.shape[0] == N_A_LAYERS and moe_w_gate.shape[0] == DEPTH
    ind, ind_t = _head_indicator(d, RWKV_HEAD)
    qk_width = diff_w_q.shape[-1]
    pr = {
        "rwkv": {
            "mix": rwkv_mix[0], "w_r": rwkv_w_rkv[0, 0].astype(BF16), "w_k": rwkv_w_rkv[0, 1].astype(BF16),
            "w_v": rwkv_w_rkv[0, 2].astype(BF16), "dw0": rwkv_decay_w0, "dw1": rwkv_decay_w1[0].astype(BF16),
            "dw2": rwkv_decay_w2[0].astype(BF16), "aw0": rwkv_a_w0, "aw1": rwkv_a_w1[0].astype(BF16),
            "aw2": rwkv_a_w2[0].astype(BF16), "gw1": rwkv_g_w1[0].astype(BF16),
            "gw2": rwkv_g_w2[0].astype(BF16), "kk_ka": rwkv_kk_ka[0], "r_k": rwkv_r_k[0].reshape(1, d),
            "lnx": rwkv_lnx[0], "w_out": rwkv_w_out[0].astype(BF16), "ind": ind, "ind_t": ind_t,
        },
        "post_ln": post_ln,
        "router_wt": router_w.T,
        "router_b": router_b.reshape(N_EXPERTS, 1),
        "moe": [(moe_w_gate[l].astype(BF16), moe_w_up[l].astype(BF16), moe_w_down[l].astype(BF16))
                for l in range(DEPTH)],
        "w_k_shared": w_kv_shared[:, :qk_width].astype(BF16),
        "w_v_shared": w_kv_shared[:, qk_width:].astype(BF16),
        "diff_w_q": diff_w_q[0].astype(BF16),
        "diff_lambda": diff_lambda[0],
        "diff_subln": diff_subln[0].reshape(1, -1),
        "diff_w_out": diff_w_out[0].astype(BF16),
    }

    bp = x_prompt.shape[0]
    n_meta = meta_tokens.shape[0]
    meta = jnp.broadcast_to(meta_tokens[None].astype(x_prompt.dtype), (bp, n_meta, d))
    xp = jnp.concatenate([meta, x_prompt], axis=1)
    wkv0 = jnp.zeros((bp,) + state_wkv.shape[2:], x_prompt.dtype)
    shift0 = jnp.zeros((bp, d), x_prompt.dtype)
    yp, k_p, v_p, wkv_p, shift_p = _run(xp, wkv0, shift0, None, pr)

    n_pool, page = cache_k.shape[:2]
    past = (cache_k.reshape(n_pool, page, d), cache_v.reshape(n_pool, page, d), page_table)
    ys, k_s, v_s, wkv_s, shift_s = _run(x_sample, state_wkv[:, 0], state_shift[:, 0], past, pr)
    return (yp[:, n_meta:], ys, k_p, v_p, wkv_p, shift_p, k_s, v_s, wkv_s, shift_s)
```

```python
import functools
import math

import jax
import jax.numpy as jnp
from jax import lax
from jax.experimental import pallas as pl
from jax.experimental.pallas import tpu as pltpu

F32 = jnp.float32
BF16 = jnp.bfloat16

DEPTH = 2
N_A_LAYERS = 1
RWKV_HEAD = 64
DIFF_HEAD = 64
LNX_EPS = 64e-5
SUBLN_EPS = 1e-5
LN_EPS = 1e-5
N_EXPERTS = 16
N_GROUPS = 4
EXPERTS_PER_GROUP = N_EXPERTS // N_GROUPS
TOP_K = 2
DEEPNORM_ALPHA = (2 * DEPTH) ** 0.25
LAMBDA_INIT = 0.8 - 0.6 * math.exp(-0.3 * 1)

LANES = 128
SUBLANES = 8
VMEM_LIMIT_BYTES = 56 * 1024 * 1024


def _cparams(*sem):
    return pltpu.CompilerParams(dimension_semantics=sem, vmem_limit_bytes=VMEM_LIMIT_BYTES)


def _row_tile(n, candidates):
    for c in candidates:
        if n % c == 0:
            return c
    raise ValueError(f"no row tile for {n} in {candidates}")


def _full(shape):
    zeros = (0,) * len(shape)
    return pl.BlockSpec(shape, lambda *_: zeros)


def _dot(a, b):
    return jnp.dot(a, b, preferred_element_type=F32)


def _dot_nt(a, b):
    return lax.dot_general(a, b, (((1,), (1,)), ((), ())), preferred_element_type=F32)


def _split_dot(x, w_bf16):
    hi = x.astype(BF16)
    lo = (x - hi.astype(F32)).astype(BF16)
    return _dot(hi, w_bf16) + _dot(lo, w_bf16)


def _layernorm(z, g, b):
    mu = jnp.mean(z, axis=-1, keepdims=True)
    zc = z - mu
    var = jnp.mean(zc * zc, axis=-1, keepdims=True)
    return zc * lax.rsqrt(var + LN_EPS) * g + b


def _head_indicator(d, head):
    n_heads = d // head
    assert n_heads <= LANES
    ind = (jnp.arange(d)[:, None] // head == jnp.arange(LANES)[None, :]).astype(BF16)
    return ind, ind.T


def _rwkv_pre_kernel(x_ref, xp_ref, mix_ref, wr_ref, wk_ref, wv_ref, dw0_ref, dw1_ref, dw2_ref,
                     aw0_ref, aw1_ref, aw2_ref, gw1_ref, gw2_ref, kkka_ref, ind_ref, indt_ref,
                     r_ref, w_ref, k_ref, v_ref, a_ref, b_ref, g_ref):
    x = x_ref[...]
    xx = xp_ref[...] - x

    def mixed(i):
        return (x + xx * mix_ref[i:i + 1, :]).astype(BF16)

    r = _dot(mixed(0), wr_ref[...])
    k = _dot(mixed(2), wk_ref[...])
    v = _dot(mixed(3), wv_ref[...])
    lw = jnp.tanh(_dot(mixed(1), dw1_ref[...])).astype(BF16)
    z = -(dw0_ref[...] + _dot(lw, dw2_ref[...]))
    softplus = jnp.maximum(z, 0.0) + jnp.log(1.0 + jnp.exp(-jnp.abs(z)))
    w_log = -softplus - 0.5
    decay = jnp.exp(-jnp.exp(w_log))
    la = _dot(mixed(4), aw1_ref[...]).astype(BF16)
    a = jax.nn.sigmoid(aw0_ref[...] + _dot(la, aw2_ref[...]))
    lg = jax.nn.sigmoid(_dot(mixed(5), gw1_ref[...])).astype(BF16)
    g = _dot(lg, gw2_ref[...])

    kk = k * kkka_ref[0:1, :]
    ss = _split_dot(_split_dot(kk * kk, ind_ref[...]), indt_ref[...])
    kk = kk / jnp.maximum(jnp.sqrt(ss), 1e-12)
    k = k * (1.0 + (a - 1.0) * kkka_ref[1:2, :])

    r_ref[...] = r
    w_ref[...] = decay
    k_ref[...] = k
    v_ref[...] = v
    a_ref[...] = -kk
    b_ref[...] = kk * a
    g_ref[...] = g


def _time_major_spec(tm, d, tiles_per_seq):
    return pl.BlockSpec((tm, d), lambda i: (i % tiles_per_seq, i // tiles_per_seq))


def _rwkv_pre(x, x_prev, p, b, t, time_major):
    n, d = x.shape
    tm = _row_tile(t, (344, 256, 128, 64, 8)) if time_major else _row_tile(n, (192, 128, 64, 8))
    row = pl.BlockSpec((tm, d), lambda i: (i, 0))
    seq_spec = _time_major_spec(tm, d, t // tm) if time_major else row
    seq_shape = (t, b * d) if time_major else (n, d)
    weights = [p["mix"], p["w_r"], p["w_k"], p["w_v"], p["dw0"], p["dw1"], p["dw2"], p["aw0"],
               p["aw1"], p["aw2"], p["gw1"], p["gw2"], p["kk_ka"], p["ind"], p["ind_t"]]
    return pl.pallas_call(
        _rwkv_pre_kernel,
        grid=(n // tm,),
        in_specs=[row, row] + [_full(w.shape) for w in weights],
        out_specs=[seq_spec] * 6 + [row],
        out_shape=[jax.ShapeDtypeStruct(seq_shape, F32)] * 6 + [jax.ShapeDtypeStruct((n, d), F32)],
        compiler_params=_cparams("parallel"),
        name="rwkv_pre",
    )(x, x_prev, *weights)


SCAN_VALUE_SPLIT = 2


def _scan_groups(s_ref):
    nv = s_ref.shape[1] // SCAN_VALUE_SPLIT
    return nv, range(0, s_ref.shape[1], nv)


def _state_times(s_ref, vec_ref):
    nv, groups = _scan_groups(s_ref)
    out = []
    for lo in groups:
        acc = [jnp.zeros((nv, LANES), F32), jnp.zeros((nv, LANES), F32)]
        for j in range(s_ref.shape[0]):
            acc[j % 2] = acc[j % 2] + s_ref[j, lo:lo + nv, :] * vec_ref[pl.ds(j, 1), :]
        out.append(acc[0] + acc[1])
    return tuple(out)


def _scan_step(s_ref, r_ref, w_ref, k_ref, v_ref, b_ref, a_next_ref, o_ref, sa_all):
    nv, groups = _scan_groups(s_ref)
    sa_next = []
    for g, lo in enumerate(groups):
        vt = v_ref[lo:lo + nv, :]
        sa = sa_all[g]
        ot = [jnp.zeros_like(vt), jnp.zeros_like(vt)]
        sn = [jnp.zeros_like(vt), jnp.zeros_like(vt)]
        for j in range(s_ref.shape[0]):
            s = (s_ref[j, lo:lo + nv, :] * w_ref[pl.ds(j, 1), :] + sa * b_ref[pl.ds(j, 1), :]
                 + vt * k_ref[pl.ds(j, 1), :])
            s_ref[j, lo:lo + nv, :] = s
            ot[j % 2] = ot[j % 2] + s * r_ref[pl.ds(j, 1), :]
            sn[j % 2] = sn[j % 2] + s * a_next_ref[pl.ds(j, 1), :]
        o_ref[lo:lo + nv, :] = ot[0] + ot[1]
        sa_next.append(sn[0] + sn[1])
    return tuple(sa_next)


def _rwkv_scan_kernel(r_ref, w_ref, k_ref, v_ref, a_ref, b_ref, s0_ref, o_ref, sout_ref, s_ref,
                      buf_a, buf_b, obuf_a, obuf_b):
    tc = pl.program_id(0)
    steps, nb, d = r_ref.shape
    n_chunks = d // LANES
    half = LANES // 2
    srcs = (r_ref, w_ref, k_ref, v_ref, a_ref, b_ref)

    @pl.when(tc == 0)
    def _():
        s_ref[...] = s0_ref[...]

    def to_lanes(t, buf):
        for i_op, src in enumerate(srcs):
            x = src[t]
            rows = jnp.concatenate([x[:, i * LANES:(i + 1) * LANES] for i in range(n_chunks)], axis=0)
            cols = rows.T
            buf[i_op] = jnp.concatenate([cols[:half], cols[half:]], axis=1)

    def from_lanes(obuf, t):
        y = obuf[...]
        rows = jnp.concatenate([y[:, :half], y[:, half:]], axis=0).T
        for i in range(n_chunks):
            o_ref[t, :, i * LANES:(i + 1) * LANES] = rows[i * nb:(i + 1) * nb, :]

    def step(cur, nxt, obuf, sa):
        r_b, w_b, k_b, v_b, _, b_b = (cur.at[i] for i in range(6))
        return _scan_step(s_ref, r_b, w_b, k_b, v_b, b_b, nxt.at[4], obuf, sa)

    def pair(i, sa):
        t = 2 * i
        to_lanes(t + 1, buf_b)
        from_lanes(obuf_b, jnp.maximum(t - 1, 0))
        sa = step(buf_a, buf_b, obuf_a, sa)
        to_lanes(jnp.minimum(t + 2, steps - 1), buf_a)
        from_lanes(obuf_a, t)
        return step(buf_b, buf_a, obuf_b, sa)

    to_lanes(0, buf_a)
    obuf_b[...] = jnp.zeros(obuf_b.shape, F32)
    lax.fori_loop(0, steps // 2, pair, _state_times(s_ref, buf_a.at[4]))
    from_lanes(obuf_b, steps - 1)

    @pl.when(tc == pl.num_programs(0) - 1)
    def _():
        sout_ref[...] = s_ref[...]


def _rwkv_scan_native_state_kernel(r_ref, w_ref, k_ref, v_ref, a_ref, b_ref, s0_ref, o_ref, sout_ref,
                                   s_ref, tmp_ref):
    steps = r_ref.shape[0]
    nk = s_ref.shape[0]
    tmp_ref[...] = s0_ref[...].T
    for j in range(nk):
        s_ref[j] = tmp_ref[pl.ds(j, nk, stride=nk), :]

    def step(t, sa):
        return _scan_step(s_ref, r_ref.at[t], w_ref.at[t], k_ref.at[t], v_ref.at[t], b_ref.at[t],
                          a_ref.at[jnp.minimum(t + 1, steps - 1)], o_ref.at[t], sa)

    lax.fori_loop(0, steps, step, _state_times(s_ref, a_ref.at[0]))
    for j in range(nk):
        tmp_ref[pl.ds(j, nk, stride=nk), :] = s_ref[j]
    sout_ref[...] = tmp_ref[...].T


def _rwkv_scan_native_state(r, w, k, v, a, b, state):
    t, nk, l = r.shape
    nb, width = state.shape
    assert nb == LANES and width == (l // LANES) * nk * nk
    seq = pl.BlockSpec((t, nk, LANES), lambda h: (0, 0, h))
    st = pl.BlockSpec((LANES, nk * nk), lambda h: (0, h))
    return pl.pallas_call(
        _rwkv_scan_native_state_kernel,
        grid=(l // LANES,),
        in_specs=[seq] * 6 + [st],
        out_specs=[seq, st],
        out_shape=[jax.ShapeDtypeStruct((t, nk, l), F32), jax.ShapeDtypeStruct(state.shape, F32)],
        scratch_shapes=[pltpu.VMEM((nk, nk, LANES), F32), pltpu.VMEM((nk * nk, LANES), F32)],
        compiler_params=_cparams("parallel"),
        name="rwkv_scan_native_state",
    )(r, w, k, v, a, b, state)


def _rwkv_scan(r, w, k, v, a, b, s0):
    t, nb, d = r.shape
    nk = RWKV_HEAD
    assert nb * (d // LANES) == LANES // 2 and d // nk == 2 * (d // LANES)
    steps = _row_tile(t, (48, 16, 8, 4, 2))
    seq = pl.BlockSpec((steps, nb, d), lambda c: (c, 0, 0))
    st = pl.BlockSpec((nk, nk, LANES), lambda c: (0, 0, 0))
    operands = pltpu.VMEM((6, nk, LANES), F32)
    readout = pltpu.VMEM((nk, LANES), F32)
    return pl.pallas_call(
        _rwkv_scan_kernel,
        grid=(t // steps,),
        in_specs=[seq] * 6 + [st],
        out_specs=[seq, st],
        out_shape=[jax.ShapeDtypeStruct((t, nb, d), F32), jax.ShapeDtypeStruct((nk, nk, LANES), F32)],
        scratch_shapes=[pltpu.VMEM((nk, nk, LANES), F32), operands, operands, readout, readout],
        compiler_params=_cparams("arbitrary"),
        name="rwkv_scan",
    )(r, w, k, v, a, b, s0)


def _rwkv_post_kernel(o_ref, r_ref, k_ref, v_ref, g_ref, x_ref, lnx_ref, rk_ref, ind_ref, indt_ref,
                      wout_ref, ln_ref, out_ref):
    ind = ind_ref[...]
    ind_t = indt_ref[...]
    inv_head = 1.0 / RWKV_HEAD
    o = o_ref[...]
    mu = _split_dot(_split_dot(o, ind) * inv_head, ind_t)
    oc = o - mu
    var = _split_dot(oc * oc, ind) * inv_head
    rstd = _split_dot(lax.rsqrt(var + LNX_EPS), ind_t)
    on = oc * rstd * lnx_ref[0:1, :] + lnx_ref[1:2, :]
    v = v_ref[...]
    bonus = _split_dot(_split_dot(r_ref[...] * k_ref[...] * rk_ref[...], ind), ind_t)
    on = on + bonus * v
    h = _dot((on * g_ref[...]).astype(BF16), wout_ref[...])
    out_ref[...] = _layernorm(DEEPNORM_ALPHA * x_ref[...] + h, ln_ref[0:1, :], ln_ref[1:2, :])


def _rwkv_post(o, r, k, v, g, x, p, ln, t, time_major):
    n, d = x.shape
    tm = _row_tile(t, (344, 256, 128, 64, 8)) if time_major else _row_tile(n, (384, 256, 128, 64, 8))
    row = pl.BlockSpec((tm, d), lambda i: (i, 0))
    seq_spec = _time_major_spec(tm, d, t // tm) if time_major else row
    weights = [p["lnx"], p["r_k"], p["ind"], p["ind_t"], p["w_out"], ln]
    return pl.pallas_call(
        _rwkv_post_kernel,
        grid=(n // tm,),
        in_specs=[seq_spec] * 4 + [row] * 2 + [_full(w.shape) for w in weights],
        out_specs=row,
        out_shape=jax.ShapeDtypeStruct((n, d), F32),
        compiler_params=_cparams("parallel"),
        name="rwkv_post",
    )(o, r, k, v, g, x, *weights)


def _proj_kernel(*refs, n_out):
    x_ref = refs[0]
    w_refs = refs[1:1 + n_out]
    out_refs = refs[1 + n_out:]
    x = x_ref[...].astype(BF16)
    for w_ref, out_ref in zip(w_refs, out_refs):
        out_ref[...] = _dot(x, w_ref[...])


def _proj(x, ws, name):
    n, d = x.shape
    tm = _row_tile(n, (384, 256, 128, 64, 8))
    return pl.pallas_call(
        functools.partial(_proj_kernel, n_out=len(ws)),
        grid=(n // tm,),
        in_specs=[pl.BlockSpec((tm, d), lambda i: (i, 0))] + [_full(w.shape) for w in ws],
        out_specs=[pl.BlockSpec((tm, w.shape[1]), lambda i: (i, 0)) for w in ws],
        out_shape=[jax.ShapeDtypeStruct((n, w.shape[1]), F32) for w in ws],
        compiler_params=_cparams("parallel"),
        name=name,
    )(x, *ws)


def _proj_ln_kernel(a_ref, x_ref, w_ref, ln_ref, out_ref):
    h = _dot(a_ref[...].astype(BF16), w_ref[...])
    out_ref[...] = _layernorm(DEEPNORM_ALPHA * x_ref[...] + h, ln_ref[0:1, :], ln_ref[1:2, :])


def _proj_ln(a, x, w, ln):
    n, d = x.shape
    tm = _row_tile(n, (384, 256, 128, 64, 8))
    row = pl.BlockSpec((tm, d), lambda i: (i, 0))
    return pl.pallas_call(
        _proj_ln_kernel,
        grid=(n // tm,),
        in_specs=[pl.BlockSpec((tm, a.shape[1]), lambda i: (i, 0)), row, _full(w.shape), _full(ln.shape)],
        out_specs=row,
        out_shape=jax.ShapeDtypeStruct((n, d), F32),
        compiler_params=_cparams("parallel"),
        name="proj_ln",
    )(a, x, w, ln)


MOE_TILE = 256


def _router_kernel(x_ref, wt_ref, bias_ref, tri_ref, eidx_ref, wts_ref, rank_ref, cnt_ref, run_ref):
    step = pl.program_id(0)

    @pl.when(step == 0)
    def _():
        run_ref[...] = jnp.zeros(run_ref.shape, F32)

    x = x_ref[...]
    x_hi = x.astype(BF16)
    x_lo = (x - x_hi.astype(F32)).astype(BF16)
    wt = wt_ref[...]
    wt_hi = wt.astype(BF16)
    wt_lo = (wt - wt_hi.astype(F32)).astype(BF16)
    logits = _dot_nt(wt_hi, x_hi) + _dot_nt(wt_hi, x_lo) + _dot_nt(wt_lo, x_hi)
    s = jax.nn.sigmoid(logits)
    sel = s + bias_ref[...]
    s_rows = [s[e:e + 1, :] for e in range(N_EXPERTS)]
    sel_rows = [sel[e:e + 1, :] for e in range(N_EXPERTS)]

    best_grp = None
    best_idx = None
    for gi in range(N_GROUPS):
        rows = sel_rows[gi * EXPERTS_PER_GROUP:(gi + 1) * EXPERTS_PER_GROUP]
        score = None
        for i in range(EXPERTS_PER_GROUP):
            for j in range(i + 1, EXPERTS_PER_GROUP):
                pair = rows[i] + rows[j]
                score = pair if score is None else jnp.maximum(score, pair)
        if best_grp is None:
            best_grp = score
            best_idx = jnp.zeros(score.shape, jnp.int32)
        else:
            better = score > best_grp
            best_grp = jnp.where(better, score, best_grp)
            best_idx = jnp.where(better, gi, best_idx)

    neg_inf = jnp.float32(-jnp.inf)
    masked = [jnp.where(best_idx == e // EXPERTS_PER_GROUP, sel_rows[e], neg_inf)
              for e in range(N_EXPERTS)]

    def argmax_rows(exclude):
        top = jnp.full(masked[0].shape, neg_inf)
        idx = jnp.full(masked[0].shape, -1, jnp.int32)
        val = jnp.zeros(masked[0].shape, F32)
        for e in range(N_EXPERTS):
            cand = masked[e] if exclude is None else jnp.where(exclude == e, neg_inf, masked[e])
            better = cand > top
            top = jnp.where(better, cand, top)
            idx = jnp.where(better, e, idx)
            val = jnp.where(better, s_rows[e], val)
        return idx, val

    i1, w1 = argmax_rows(None)
    i2, w2 = argmax_rows(i1)
    tot = w1 + w2
    eidx_ref[0:1, :] = i1
    eidx_ref[1:2, :] = i2
    wts_ref[0:1, :] = w1 / tot
    wts_ref[1:2, :] = w2 / tot

    e_iota = lax.broadcasted_iota(jnp.int32, logits.shape, 0)
    run = run_ref[...]
    for slot, idx in enumerate((i1, i2)):
        onehot = jnp.where(e_iota == idx, 1.0, 0.0)
        prefix = _dot(onehot.astype(BF16), tri_ref[...])
        rank = jnp.sum(onehot * (run[:, 0:1] + prefix - 1.0), axis=0, keepdims=True)
        rank_ref[slot:slot + 1, :] = rank.astype(jnp.int32)
        run = run + jnp.sum(onehot, axis=-1, keepdims=True)
    run_ref[...] = run
    cnt_ref[...] = run


def _router(x, router_wt, router_b):
    n, d = x.shape
    tm = _row_tile(n, (384, 256, 128))
    tri = (jnp.arange(tm)[:, None] <= jnp.arange(tm)[None, :]).astype(BF16)
    pair = pl.BlockSpec((TOP_K, tm), lambda i: (0, i))
    return pl.pallas_call(
        _router_kernel,
        grid=(n // tm,),
        in_specs=[pl.BlockSpec((tm, d), lambda i: (i, 0)), _full(router_wt.shape), _full(router_b.shape),
                  _full(tri.shape)],
        out_specs=[pair, pair, pair, _full((N_EXPERTS, LANES))],
        out_shape=[jax.ShapeDtypeStruct((TOP_K, n), jnp.int32), jax.ShapeDtypeStruct((TOP_K, n), F32),
                   jax.ShapeDtypeStruct((TOP_K, n), jnp.int32), jax.ShapeDtypeStruct((N_EXPERTS, LANES), F32)],
        scratch_shapes=[pltpu.VMEM((N_EXPERTS, LANES), F32)],
        compiler_params=_cparams("arbitrary"),
        name="router",
    )(x, router_wt, router_b, tri)


DMA_UNROLL = 8


def _dispatch_kernel(pos_ref, x_ref, init_hbm, xs_hbm, sem):
    del init_hbm
    tm = x_ref.shape[0]

    def row_copy(slot, r):
        return pltpu.make_async_copy(x_ref.at[pl.ds(r, 1)], xs_hbm.at[pl.ds(pos_ref[0, slot, r], 1)], sem)

    def start(r, c):
        for slot in range(TOP_K):
            row_copy(slot, r).start()
        return c

    def wait(r, c):
        for slot in range(TOP_K):
            row_copy(slot, r).wait()
        return c

    lax.fori_loop(0, tm, start, 0, unroll=DMA_UNROLL)
    lax.fori_loop(0, tm, wait, 0, unroll=DMA_UNROLL)


def _dispatch(x, pos_blocks, n_rows):
    n, d = x.shape
    nt, _, tm = pos_blocks.shape
    return pl.pallas_call(
        _dispatch_kernel,
        grid=(nt,),
        in_specs=[pl.BlockSpec((1, TOP_K, tm), lambda i: (i, 0, 0), memory_space=pltpu.SMEM),
                  pl.BlockSpec((tm, d), lambda i: (i, 0)),
                  pl.BlockSpec(memory_space=pl.ANY)],
        out_specs=pl.BlockSpec(memory_space=pl.ANY),
        out_shape=jax.ShapeDtypeStruct((n_rows, d), F32),
        scratch_shapes=[pltpu.SemaphoreType.DMA(())],
        input_output_aliases={2: 0},
        compiler_params=_cparams("arbitrary"),
        name="moe_dispatch",
    )(pos_blocks, x, jnp.zeros((n_rows, d), F32))


def _experts_kernel(tile_e_ref, n_used_ref, xs_ref, wg_ref, wu_ref, wd_ref, ys_ref):
    del tile_e_ref
    i = pl.program_id(0)

    @pl.when(i < n_used_ref[0])
    def _():
        x = xs_ref[...].astype(BF16)
        hg = _dot(x, wg_ref[0])
        hu = _dot(x, wu_ref[0])
        h = (hg * jax.nn.sigmoid(hg) * hu).astype(BF16)
        ys_ref[...] = _dot(h, wd_ref[0])

    @pl.when(i >= n_used_ref[0])
    def _():
        ys_ref[...] = jnp.zeros(ys_ref.shape, F32)


def _experts(xs, tile_e, n_used, wg, wu, wd):
    n_rows, d = xs.shape
    d_exp = wg.shape[2]
    tm = MOE_TILE

    def w_map(i, te, nu):
        return (te[i], 0, 0)

    return pl.pallas_call(
        _experts_kernel,
        grid_spec=pltpu.PrefetchScalarGridSpec(
            num_scalar_prefetch=2,
            grid=(n_rows // tm,),
            in_specs=[
                pl.BlockSpec((tm, d), lambda i, te, nu: (jnp.minimum(i, nu[0] - 1), 0)),
                pl.BlockSpec((1, d, d_exp), w_map),
                pl.BlockSpec((1, d, d_exp), w_map),
                pl.BlockSpec((1, d_exp, d), w_map),
            ],
            out_specs=pl.BlockSpec((tm, d), lambda i, te, nu: (i, 0)),
        ),
        out_shape=jax.ShapeDtypeStruct((n_rows, d), F32),
        compiler_params=_cparams("arbitrary"),
        name="moe_experts",
    )(tile_e, n_used, xs, wg, wu, wd)


def _combine_ln_kernel(pos_ref, wts_ref, x_ref, ln_ref, ys_hbm, out_ref, ybuf, sem):
    tm = x_ref.shape[0]

    def row_copy(slot, r):
        return pltpu.make_async_copy(ys_hbm.at[pl.ds(pos_ref[0, slot, r], 1)], ybuf.at[slot, pl.ds(r, 1)], sem)

    def start(r, c):
        for slot in range(TOP_K):
            row_copy(slot, r).start()
        return c

    def wait(r, c):
        for slot in range(TOP_K):
            row_copy(slot, r).wait()
        return c

    lax.fori_loop(0, tm, start, 0, unroll=DMA_UNROLL)
    lax.fori_loop(0, tm, wait, 0, unroll=DMA_UNROLL)
    y = wts_ref[:, 0:1] * ybuf[0] + wts_ref[:, 1:2] * ybuf[1]
    out_ref[...] = _layernorm(DEEPNORM_ALPHA * x_ref[...] + y, ln_ref[0:1, :], ln_ref[1:2, :])


def _combine_ln(ys, pos_blocks, wts_t, x, ln):
    n, d = x.shape
    nt, _, tm = pos_blocks.shape
    row = pl.BlockSpec((tm, d), lambda i: (i, 0))
    return pl.pallas_call(
        _combine_ln_kernel,
        grid=(nt,),
        in_specs=[pl.BlockSpec((1, TOP_K, tm), lambda i: (i, 0, 0), memory_space=pltpu.SMEM),
                  pl.BlockSpec((tm, TOP_K), lambda i: (i, 0)), row, _full(ln.shape),
                  pl.BlockSpec(memory_space=pl.ANY)],
        out_specs=row,
        out_shape=jax.ShapeDtypeStruct((n, d), F32),
        scratch_shapes=[pltpu.VMEM((TOP_K, tm, d), F32), pltpu.SemaphoreType.DMA(())],
        compiler_params=_cparams("arbitrary"),
        name="moe_combine_ln",
    )(pos_blocks, wts_t, x, ln, ys)


def _moe_ln(x, ln, router_wt, router_b, wg, wu, wd):
    n, d = x.shape
    eidx, wts, rank, cnt = _router(x, router_wt, router_b)

    n_tiles = pl.cdiv(TOP_K * n, MOE_TILE) + N_EXPERTS
    counts = cnt[:, 0].astype(jnp.int32)
    tiles_per_e = (counts + MOE_TILE - 1) // MOE_TILE
    tile_end = jnp.cumsum(tiles_per_e)
    tile_start = tile_end - tiles_per_e
    n_used = tile_end[-1:]
    experts = jnp.arange(N_EXPERTS, dtype=jnp.int32)
    base = jnp.sum(jnp.where(eidx[..., None] == experts, tile_start * MOE_TILE, 0), axis=-1)
    pos = base + rank
    tile_ids = jnp.minimum(jnp.arange(n_tiles, dtype=jnp.int32), n_used - 1)
    tile_e = jnp.sum((tile_end[None, :] <= tile_ids[:, None]).astype(jnp.int32), axis=1)
    tile_e = jnp.minimum(tile_e, N_EXPERTS - 1)

    tm = _row_tile(n, (384, 256, 128))
    pos_blocks = jnp.transpose(pos.reshape(TOP_K, n // tm, tm), (1, 0, 2))
    xs = _dispatch(x, pos_blocks, n_tiles * MOE_TILE)
    ys = _experts(xs, tile_e, n_used, wg, wu, wd)
    return _combine_ln(ys, pos_blocks, wts.T, x, ln)


def _lambda_value(lam_ref):
    lf = lam_ref[...]
    s01 = jnp.sum(lf[0:1, :] * lf[1:2, :], axis=-1, keepdims=True)
    s23 = jnp.sum(lf[2:3, :] * lf[3:4, :], axis=-1, keepdims=True)
    return jnp.exp(s01) - jnp.exp(s23) + LAMBDA_INIT


def _sub_ln(o, subln):
    o = o * lax.rsqrt(jnp.mean(o * o, axis=-1, keepdims=True) + SUBLN_EPS) * subln
    return o * (1.0 - LAMBDA_INIT)


def _split_maps(q):
    lane = lax.broadcasted_iota(jnp.int32, q.shape, 1)
    q0 = jnp.where(lane < DIFF_HEAD, q, 0.0).astype(BF16)
    q1 = jnp.where(lane >= DIFF_HEAD, q, 0.0).astype(BF16)
    return q0, q1


def _attn_prompt_kernel(lam_ref, subln_ref, q_ref, k_ref, v_ref, o_ref):
    qi = pl.program_id(2)
    tq = q_ref.shape[1]
    q0, q1 = _split_maps(q_ref[0] * (DIFF_HEAD ** -0.5))

    def chunk(j, carry, visible):
        start = pl.multiple_of(j * tq, SUBLANES)
        kc = k_ref[0, pl.ds(start, tq), :].astype(BF16)
        vc = v_ref[0, pl.ds(start, tq), :].astype(BF16)
        out = []
        for qm, (m, l, acc) in zip((q0, q1), carry):
            s = _dot_nt(qm, kc)
            if visible is not None:
                s = jnp.where(visible, s, -jnp.inf)
            m_new = jnp.maximum(m, jnp.max(s, axis=-1, keepdims=True))
            p = jnp.exp(s - m_new)
            scale = jnp.exp(m - m_new)
            l_new = scale * l + jnp.sum(p, axis=-1, keepdims=True)
            acc_new = scale * acc + _dot(p.astype(BF16), vc)
            out.append((m_new, l_new, acc_new))
        return tuple(out)

    init = (jnp.full((tq, 1), -jnp.inf, F32), jnp.zeros((tq, 1), F32), jnp.zeros((tq, 2 * DIFF_HEAD), F32))
    carry = lax.fori_loop(0, qi, functools.partial(chunk, visible=None), (init, init))
    causal = (lax.broadcasted_iota(jnp.int32, (tq, tq), 1) <= lax.broadcasted_iota(jnp.int32, (tq, tq), 0))
    (_, l0, a0), (_, l1, a1) = chunk(qi, carry, causal)
    o = a0 / l0 - _lambda_value(lam_ref) * (a1 / l1)
    o_ref[0] = _sub_ln(o, subln_ref[...])


def _attn_prompt(q, k, v, lam_vecs, subln):
    b, t, d = q.shape
    hw = 2 * DIFF_HEAD
    tq = _row_tile(t, (344, 256, 128, 64, 8))
    return pl.pallas_call(
        _attn_prompt_kernel,
        grid=(b, d // hw, t // tq),
        in_specs=[
            _full(lam_vecs.shape),
            _full(subln.shape),
            pl.BlockSpec((1, tq, hw), lambda bi, h, qi: (bi, qi, h)),
            pl.BlockSpec((1, t, hw), lambda bi, h, qi: (bi, 0, h)),
            pl.BlockSpec((1, t, hw), lambda bi, h, qi: (bi, 0, h)),
        ],
        out_specs=pl.BlockSpec((1, tq, hw), lambda bi, h, qi: (bi, qi, h)),
        out_shape=jax.ShapeDtypeStruct((b, t, d), F32),
        compiler_params=_cparams("parallel", "parallel", "arbitrary"),
        name="attn_prompt",
    )(lam_vecs, subln, q, k, v)


PAGES_PER_STEP = 4


def _attn_decode_kernel(pt_ref, lam_ref, subln_ref, q_ref, *refs):
    del pt_ref
    g = PAGES_PER_STEP
    kc_refs, vc_refs = refs[0:g], refs[g:2 * g]
    kn_ref, vn_ref, o_ref, qq_ref, m_ref, l_ref, acc_ref = refs[2 * g:]
    p = pl.program_id(1)
    tq = q_ref.shape[1]
    page = kc_refs[0].shape[1]
    hw = 2 * DIFF_HEAD
    n_heads = q_ref.shape[2] // hw
    rows = 2 * tq

    @pl.when(p == 0)
    def _():
        m_ref[...] = jnp.full(m_ref.shape, -jnp.inf, F32)
        l_ref[...] = jnp.zeros(l_ref.shape, F32)
        acc_ref[...] = jnp.zeros(acc_ref.shape, F32)
        for h in range(n_heads):
            q0, q1 = _split_maps(q_ref[0, :, h * hw:(h + 1) * hw] * (DIFF_HEAD ** -0.5))
            qq_ref[h] = jnp.concatenate([q0, q1], axis=0)

    def update(k_blocks, v_blocks, visible):
        nb = len(k_blocks)
        s = [jnp.concatenate([_dot_nt(qq_ref[h], k_blocks[j][h]) for h in range(n_heads)], axis=0)
             for j in range(nb)]
        if visible is not None:
            s = [jnp.where(visible, sj, -jnp.inf) for sj in s]
        m_old = m_ref[...]
        m_new = jnp.maximum(m_old, jnp.max(functools.reduce(jnp.maximum, s), axis=-1, keepdims=True))
        pr = [jnp.exp(sj - m_new) for sj in s]
        scale = jnp.exp(m_old - m_new)
        l_ref[...] = scale * l_ref[...] + jnp.sum(functools.reduce(jnp.add, pr), axis=-1, keepdims=True)
        pr = [pj.astype(BF16) for pj in pr]
        pv = []
        for h in range(n_heads):
            acc_h = None
            for j in range(nb):
                d = _dot(pr[j][h * rows:(h + 1) * rows, :], v_blocks[j][h])
                acc_h = d if acc_h is None else acc_h + d
            pv.append(acc_h)
        acc_ref[...] = scale * acc_ref[...] + jnp.concatenate(pv, axis=0)
        m_ref[...] = m_new

    def head_slices(ref):
        return [ref[0, :, h * hw:(h + 1) * hw].astype(BF16) for h in range(n_heads)]

    update([head_slices(r) for r in kc_refs], [head_slices(r) for r in vc_refs], None)

    @pl.when(p == pl.num_programs(1) - 1)
    def _():
        q_idx = lax.broadcasted_iota(jnp.int32, (n_heads * rows, page), 0) % tq
        k_idx = lax.broadcasted_iota(jnp.int32, (n_heads * rows, page), 1)
        pad = jnp.zeros((page - tq, hw), BF16)
        update([[jnp.concatenate([kh, pad], axis=0) for kh in head_slices(kn_ref)]],
               [[jnp.concatenate([vh, pad], axis=0) for vh in head_slices(vn_ref)]], k_idx <= q_idx)
        lam = _lambda_value(lam_ref)
        on = acc_ref[...] / l_ref[...]
        for h in range(n_heads):
            o = on[h * rows:h * rows + tq, :] - lam * on[h * rows + tq:(h + 1) * rows, :]
            o_ref[0, :, h * hw:(h + 1) * hw] = _sub_ln(o, subln_ref[...])


def _attn_decode(q, cache_k, cache_v, page_table, k_new, v_new, lam_vecs, subln):
    db, tq, d = q.shape
    n_pages = page_table.shape[1]
    page = cache_k.shape[1]
    hw = 2 * DIFF_HEAD
    n_heads = d // hw
    g = PAGES_PER_STEP
    assert tq % SUBLANES == 0 and tq <= page and page == hw and n_pages % g == 0

    def page_map(j):
        return lambda bi, p, pt: (pt[bi * n_pages + p * g + j], 0, 0)

    def batch_map(bi, p, pt):
        return (bi, 0, 0)

    page_specs = [pl.BlockSpec((1, page, d), page_map(j)) for j in range(g)]
    stat = pltpu.VMEM((n_heads * 2 * tq, hw), F32)
    return pl.pallas_call(
        _attn_decode_kernel,
        grid_spec=pltpu.PrefetchScalarGridSpec(
            num_scalar_prefetch=1,
            grid=(db, n_pages // g),
            in_specs=[
                pl.BlockSpec(lam_vecs.shape, lambda bi, p, pt: (0, 0)),
                pl.BlockSpec(subln.shape, lambda bi, p, pt: (0, 0)),
                pl.BlockSpec((1, tq, d), batch_map),
                *page_specs,
                *page_specs,
                pl.BlockSpec((1, tq, d), batch_map),
                pl.BlockSpec((1, tq, d), batch_map),
            ],
            out_specs=pl.BlockSpec((1, tq, d), batch_map),
            scratch_shapes=[pltpu.VMEM((n_heads, 2 * tq, hw), BF16), stat, stat, stat],
        ),
        out_shape=jax.ShapeDtypeStruct((db, tq, d), F32),
        compiler_params=_cparams("parallel", "arbitrary"),
        name="attn_decode",
    )(page_table.reshape(-1), lam_vecs, subln, q, *([cache_k] * g), *([cache_v] * g), k_new, v_new)


def _to_head_batch_lanes(z, b, seq, heads):
    z = z.reshape(b, seq, heads, RWKV_HEAD)
    return jnp.transpose(z, (1, 3, 2, 0)).reshape(seq, RWKV_HEAD, heads * b)


def _from_head_batch_lanes(z, b, seq, heads):
    z = z.reshape(seq, RWKV_HEAD, heads, b)
    return jnp.transpose(z, (3, 0, 2, 1)).reshape(b * seq, heads * RWKV_HEAD)


def _run(x, wkv_in, shift_in, past, pr):
    b, t, d = x.shape
    n = b * t
    heads = d // RWKV_HEAD
    xf = x.reshape(n, d)

    x_prev = jnp.concatenate([shift_in[:, None, :], x[:, :-1]], axis=1).reshape(n, d)
    time_major = b != LANES
    r, w, k, v, a, bb, g = _rwkv_pre(xf, x_prev, pr["rwkv"], b, t, time_major)
    if not time_major:
        scan_ops = [_to_head_batch_lanes(z, b, t, heads) for z in (r, w, k, v, a, bb)]
        o_l, s_fin = _rwkv_scan_native_state(*scan_ops, wkv_in.reshape(b, -1))
        o = _from_head_batch_lanes(o_l, b, t, heads)
        new_wkv = s_fin.reshape(wkv_in.shape)[:, None]
    else:
        s0 = jnp.transpose(wkv_in.reshape(b, heads // 2, 2, RWKV_HEAD, RWKV_HEAD), (4, 3, 2, 1, 0))
        o3, s_fin = _rwkv_scan(*[z.reshape(t, b, d) for z in (r, w, k, v, a, bb)],
                               s0.reshape(RWKV_HEAD, RWKV_HEAD, heads * b))
        o = o3.reshape(t, b * d)
        s_fin = s_fin.reshape(RWKV_HEAD, RWKV_HEAD, 2, heads // 2, b)
        new_wkv = jnp.transpose(s_fin, (4, 3, 2, 1, 0)).reshape(wkv_in.shape)[:, None]
    new_shift = x[:, -1][:, None]
    x1 = _rwkv_post(o, r, k, v, g, xf, pr["rwkv"], pr["post_ln"][0][0], t, time_major)
    x2 = _moe_ln(x1, pr["post_ln"][0][1], pr["router_wt"], pr["router_b"], *pr["moe"][0])

    k_new, v_new, q = _proj(x2, [pr["w_k_shared"], pr["w_v_shared"], pr["diff_w_q"]], "kv_q_proj")
    k3, v3, q3 = (z.reshape(b, t, d) for z in (k_new, v_new, q))
    if past is None:
        att = _attn_prompt(q3, k3, v3, pr["diff_lambda"], pr["diff_subln"])
    else:
        cache_k, cache_v, page_table = past
        att = _attn_decode(q3, cache_k, cache_v, page_table, k3, v3, pr["diff_lambda"], pr["diff_subln"])
    x3 = _proj_ln(att.reshape(n, d), x2, pr["diff_w_out"], pr["post_ln"][1][0])
    x4 = _moe_ln(x3, pr["post_ln"][1][1], pr["router_wt"], pr["router_b"], *pr["moe"][1])

    n_diff = d // (2 * DIFF_HEAD)
    return (x4.reshape(b, t, d), k_new.reshape(b, t, 2 * n_diff, DIFF_HEAD),
            v_new.reshape(b, t, n_diff, 2 * DIFF_HEAD), new_wkv, new_shift)


def kernel(x_prompt, x_sample, cache_k, cache_v, page_table, state_wkv, state_shift, meta_tokens,
           rwkv_mix, rwkv_w_rkv, rwkv_decay_w0, rwkv_decay_w1, rwkv_decay_w2, rwkv_a_w0, rwkv_a_w1,
           rwkv_a_w2, rwkv_g_w1, rwkv_g_w2, rwkv_kk_ka, rwkv_r_k, rwkv_lnx, rwkv_w_out, w_kv_shared,
           diff_w_q, diff_lambda, diff_subln, diff_w_out, router_w, router_b, moe_w_gate, moe_w_up,
           moe_w_down, post_ln):
    d = x_prompt.shape[-1]
    assert rwkv_mix.shape[0] == N_A_LAYERS and moe_w_gate.shape[0] == DEPTH
    ind, ind_t = _head_indicator(d, RWKV_HEAD)
    qk_width = diff_w_q.shape[-1]
    pr = {
        "rwkv": {
            "mix": rwkv_mix[0], "w_r": rwkv_w_rkv[0, 0].astype(BF16), "w_k": rwkv_w_rkv[0, 1].astype(BF16),
            "w_v": rwkv_w_rkv[0, 2].astype(BF16), "dw0": rwkv_decay_w0, "dw1": rwkv_decay_w1[0].astype(BF16),
            "dw2": rwkv_decay_w2[0].astype(BF16), "aw0": rwkv_a_w0, "aw1": rwkv_a_w1[0].astype(BF16),
            "aw2": rwkv_a_w2[0].astype(BF16), "gw1": rwkv_g_w1[0].astype(BF16),
            "gw2": rwkv_g_w2[0].astype(BF16), "kk_ka": rwkv_kk_ka[0], "r_k": rwkv_r_k[0].reshape(1, d),
            "lnx": rwkv_lnx[0], "w_out": rwkv_w_out[0].astype(BF16), "ind": ind, "ind_t": ind_t,
        },
        "post_ln": post_ln,
        "router_wt": router_w.T,
        "router_b": router_b.reshape(N_EXPERTS, 1),
        "moe": [(moe_w_gate[l].astype(BF16), moe_w_up[l].astype(BF16), moe_w_down[l].astype(BF16))
                for l in range(DEPTH)],
        "w_k_shared": w_kv_shared[:, :qk_width].astype(BF16),
        "w_v_shared": w_kv_shared[:, qk_width:].astype(BF16),
        "diff_w_q": diff_w_q[0].astype(BF16),
        "diff_lambda": diff_lambda[0],
        "diff_subln": diff_subln[0].reshape(1, -1),
        "diff_w_out": diff_w_out[0].astype(BF16),
    }

    bp = x_prompt.shape[0]
    n_meta = meta_tokens.shape[0]
    meta = jnp.broadcast_to(meta_tokens[None].astype(x_prompt.dtype), (bp, n_meta, d))
    xp = jnp.concatenate([meta, x_prompt], axis=1)
    wkv0 = jnp.zeros((bp,) + state_wkv.shape[2:], x_prompt.dtype)
    shift0 = jnp.zeros((bp, d), x_prompt.dtype)
    yp, k_p, v_p, wkv_p, shift_p = _run(xp, wkv0, shift0, None, pr)

    n_pool, page = cache_k.shape[:2]
    past = (cache_k.reshape(n_pool, page, d), cache_v.reshape(n_pool, page, d), page_table)
    ys, k_s, v_s, wkv_s, shift_s = _run(x_sample, state_wkv[:, 0], state_shift[:, 0], past, pr)
    return (yp[:, n_meta:], ys, k_p, v_p, wkv_p, shift_p, k_s, v_s, wkv_s, shift_s)
```

```python
import functools
import math

import jax
import jax.numpy as jnp
from jax import lax
from jax.experimental import pallas as pl
from jax.experimental.pallas import tpu as pltpu

F32 = jnp.float32
BF16 = jnp.bfloat16

DEPTH = 2
N_A_LAYERS = 1
RWKV_HEAD = 64
DIFF_HEAD = 64
LNX_EPS = 64e-5
SUBLN_EPS = 1e-5
LN_EPS = 1e-5
N_EXPERTS = 16
N_GROUPS = 4
EXPERTS_PER_GROUP = N_EXPERTS // N_GROUPS
TOP_K = 2
DEEPNORM_ALPHA = (2 * DEPTH) ** 0.25
LAMBDA_INIT = 0.8 - 0.6 * math.exp(-0.3 * 1)

LANES = 128
SUBLANES = 8
VMEM_LIMIT_BYTES = 56 * 1024 * 1024


def _cparams(*sem):
    return pltpu.CompilerParams(dimension_semantics=sem, vmem_limit_bytes=VMEM_LIMIT_BYTES)


def _row_tile(n, candidates):
    for c in candidates:
        if n % c == 0:
            return c
    raise ValueError(f"no row tile for {n} in {candidates}")


def _full(shape):
    zeros = (0,) * len(shape)
    return pl.BlockSpec(shape, lambda *_: zeros)


def _dot(a, b):
    return jnp.dot(a, b, preferred_element_type=F32)


def _dot_nt(a, b):
    return lax.dot_general(a, b, (((1,), (1,)), ((), ())), preferred_element_type=F32)


def _split_dot(x, w_bf16):
    hi = x.astype(BF16)
    lo = (x - hi.astype(F32)).astype(BF16)
    return _dot(hi, w_bf16) + _dot(lo, w_bf16)


def _layernorm(z, g, b):
    mu = jnp.mean(z, axis=-1, keepdims=True)
    zc = z - mu
    var = jnp.mean(zc * zc, axis=-1, keepdims=True)
    return zc * lax.rsqrt(var + LN_EPS) * g + b


def _head_indicator(d, head):
    n_heads = d // head
    assert n_heads <= LANES
    ind = (jnp.arange(d)[:, None] // head == jnp.arange(LANES)[None, :]).astype(BF16)
    return ind, ind.T


def _rwkv_pre_kernel(x_ref, xp_ref, mix_ref, wr_ref, wk_ref, wv_ref, dw0_ref, dw1_ref, dw2_ref,
                     aw0_ref, aw1_ref, aw2_ref, gw1_ref, gw2_ref, kkka_ref, ind_ref, indt_ref,
                     r_ref, w_ref, k_ref, v_ref, a_ref, b_ref, g_ref):
    x = x_ref[...]
    xx = xp_ref[...] - x

    def mixed(i):
        return (x + xx * mix_ref[i:i + 1, :]).astype(BF16)

    r = _dot(mixed(0), wr_ref[...])
    k = _dot(mixed(2), wk_ref[...])
    v = _dot(mixed(3), wv_ref[...])
    lw = jnp.tanh(_dot(mixed(1), dw1_ref[...])).astype(BF16)
    z = -(dw0_ref[...] + _dot(lw, dw2_ref[...]))
    softplus = jnp.maximum(z, 0.0) + jnp.log(1.0 + jnp.exp(-jnp.abs(z)))
    w_log = -softplus - 0.5
    decay = jnp.exp(-jnp.exp(w_log))
    la = _dot(mixed(4), aw1_ref[...]).astype(BF16)
    a = jax.nn.sigmoid(aw0_ref[...] + _dot(la, aw2_ref[...]))
    lg = jax.nn.sigmoid(_dot(mixed(5), gw1_ref[...])).astype(BF16)
    g = _dot(lg, gw2_ref[...])

    kk = k * kkka_ref[0:1, :]
    ss = _split_dot(_split_dot(kk * kk, ind_ref[...]), indt_ref[...])
    kk = kk / jnp.maximum(jnp.sqrt(ss), 1e-12)
    k = k * (1.0 + (a - 1.0) * kkka_ref[1:2, :])

    r_ref[...] = r
    w_ref[...] = decay
    k_ref[...] = k
    v_ref[...] = v
    a_ref[...] = -kk
    b_ref[...] = kk * a
    g_ref[...] = g


def _time_major_spec(tm, d, tiles_per_seq):
    return pl.BlockSpec((tm, d), lambda i: (i % tiles_per_seq, i // tiles_per_seq))


def _rwkv_pre(x, x_prev, p, b, t, time_major):
    n, d = x.shape
    tm = _row_tile(t, (344, 256, 128, 64, 8)) if time_major else _row_tile(n, (192, 128, 64, 8))
    row = pl.BlockSpec((tm, d), lambda i: (i, 0))
    seq_spec = _time_major_spec(tm, d, t // tm) if time_major else row
    seq_shape = (t, b * d) if time_major else (n, d)
    weights = [p["mix"], p["w_r"], p["w_k"], p["w_v"], p["dw0"], p["dw1"], p["dw2"], p["aw0"],
               p["aw1"], p["aw2"], p["gw1"], p["gw2"], p["kk_ka"], p["ind"], p["ind_t"]]
    return pl.pallas_call(
        _rwkv_pre_kernel,
        grid=(n // tm,),
        in_specs=[row, row] + [_full(w.shape) for w in weights],
        out_specs=[seq_spec] * 6 + [row],
        out_shape=[jax.ShapeDtypeStruct(seq_shape, F32)] * 6 + [jax.ShapeDtypeStruct((n, d), F32)],
        compiler_params=_cparams("parallel"),
        name="rwkv_pre",
    )(x, x_prev, *weights)


SCAN_VALUE_SPLIT = 2


def _scan_groups(s_ref):
    nv = s_ref.shape[1] // SCAN_VALUE_SPLIT
    return nv, range(0, s_ref.shape[1], nv)


def _state_times(s_ref, vec_ref):
    nv, groups = _scan_groups(s_ref)
    out = []
    for lo in groups:
        acc = [jnp.zeros((nv, LANES), F32), jnp.zeros((nv, LANES), F32)]
        for j in range(s_ref.shape[0]):
            acc[j % 2] = acc[j % 2] + s_ref[j, lo:lo + nv, :] * vec_ref[pl.ds(j, 1), :]
        out.append(acc[0] + acc[1])
    return tuple(out)


def _scan_step(s_ref, r_ref, w_ref, k_ref, v_ref, b_ref, a_next_ref, o_ref, sa_all):
    nv, groups = _scan_groups(s_ref)
    sa_next = []
    for g, lo in enumerate(groups):
        vt = v_ref[lo:lo + nv, :]
        sa = sa_all[g]
        ot = [jnp.zeros_like(vt), jnp.zeros_like(vt)]
        sn = [jnp.zeros_like(vt), jnp.zeros_like(vt)]
        for j in range(s_ref.shape[0]):
            s = (s_ref[j, lo:lo + nv, :] * w_ref[pl.ds(j, 1), :] + sa * b_ref[pl.ds(j, 1), :]
                 + vt * k_ref[pl.ds(j, 1), :])
            s_ref[j, lo:lo + nv, :] = s
            ot[j % 2] = ot[j % 2] + s * r_ref[pl.ds(j, 1), :]
            sn[j % 2] = sn[j % 2] + s * a_next_ref[pl.ds(j, 1), :]
        o_ref[lo:lo + nv, :] = ot[0] + ot[1]
        sa_next.append(sn[0] + sn[1])
    return tuple(sa_next)


def _rwkv_scan_kernel(r_ref, w_ref, k_ref, v_ref, a_ref, b_ref, s0_ref, o_ref, sout_ref, s_ref,
                      buf_a, buf_b, obuf_a, obuf_b):
    tc = pl.program_id(0)
    steps, nb, d = r_ref.shape
    n_chunks = d // LANES
    half = LANES // 2
    srcs = (r_ref, w_ref, k_ref, v_ref, a_ref, b_ref)

    @pl.when(tc == 0)
    def _():
        s_ref[...] = s0_ref[...]

    def to_lanes(t, buf):
        for i_op, src in enumerate(srcs):
            x = src[t]
            rows = jnp.concatenate([x[:, i * LANES:(i + 1) * LANES] for i in range(n_chunks)], axis=0)
            cols = rows.T
            buf[i_op] = jnp.concatenate([cols[:half], cols[half:]], axis=1)

    def from_lanes(obuf, t):
        y = obuf[...]
        rows = jnp.concatenate([y[:, :half], y[:, half:]], axis=0).T
        for i in range(n_chunks):
            o_ref[t, :, i * LANES:(i + 1) * LANES] = rows[i * nb:(i + 1) * nb, :]

    def step(cur, nxt, obuf, sa):
        r_b, w_b, k_b, v_b, _, b_b = (cur.at[i] for i in range(6))
        return _scan_step(s_ref, r_b, w_b, k_b, v_b, b_b, nxt.at[4], obuf, sa)

    def pair(i, sa):
        t = 2 * i
        to_lanes(t + 1, buf_b)
        from_lanes(obuf_b, jnp.maximum(t - 1, 0))
        sa = step(buf_a, buf_b, obuf_a, sa)
        to_lanes(jnp.minimum(t + 2, steps - 1), buf_a)
        from_lanes(obuf_a, t)
        return step(buf_b, buf_a, obuf_b, sa)

    to_lanes(0, buf_a)
    obuf_b[...] = jnp.zeros(obuf_b.shape, F32)
    lax.fori_loop(0, steps // 2, pair, _state_times(s_ref, buf_a.at[4]))
    from_lanes(obuf_b, steps - 1)

    @pl.when(tc == pl.num_programs(0) - 1)
    def _():
        sout_ref[...] = s_ref[...]


def _rwkv_scan_native_state_kernel(r_ref, w_ref, k_ref, v_ref, a_ref, b_ref, s0_ref, o_ref, sout_ref,
                                   s_ref, tmp_ref):
    steps = r_ref.shape[0]
    nk = s_ref.shape[0]
    tmp_ref[...] = s0_ref[...].T
    for j in range(nk):
        s_ref[j] = tmp_ref[pl.ds(j, nk, stride=nk), :]

    def step(t, sa):
        return _scan_step(s_ref, r_ref.at[t], w_ref.at[t], k_ref.at[t], v_ref.at[t], b_ref.at[t],
                          a_ref.at[jnp.minimum(t + 1, steps - 1)], o_ref.at[t], sa)

    lax.fori_loop(0, steps, step, _state_times(s_ref, a_ref.at[0]))
    for j in range(nk):
        tmp_ref[pl.ds(j, nk, stride=nk), :] = s_ref[j]
    sout_ref[...] = tmp_ref[...].T


def _rwkv_scan_native_state(r, w, k, v, a, b, state):
    t, nk, l = r.shape
    nb, width = state.shape
    assert nb == LANES and width == (l // LANES) * nk * nk
    seq = pl.BlockSpec((t, nk, LANES), lambda h: (0, 0, h))
    st = pl.BlockSpec((LANES, nk * nk), lambda h: (0, h))
    return pl.pallas_call(
        _rwkv_scan_native_state_kernel,
        grid=(l // LANES,),
        in_specs=[seq] * 6 + [st],
        out_specs=[seq, st],
        out_shape=[jax.ShapeDtypeStruct((t, nk, l), F32), jax.ShapeDtypeStruct(state.shape, F32)],
        scratch_shapes=[pltpu.VMEM((nk, nk, LANES), F32), pltpu.VMEM((nk * nk, LANES), F32)],
        compiler_params=_cparams("parallel"),
        name="rwkv_scan_native_state",
    )(r, w, k, v, a, b, state)


def _rwkv_scan(r, w, k, v, a, b, s0):
    t, nb, d = r.shape
    nk = RWKV_HEAD
    assert nb * (d // LANES) == LANES // 2 and d // nk == 2 * (d // LANES)
    steps = _row_tile(t, (48, 16, 8, 4, 2))
    seq = pl.BlockSpec((steps, nb, d), lambda c: (c, 0, 0))
    st = pl.BlockSpec((nk, nk, LANES), lambda c: (0, 0, 0))
    operands = pltpu.VMEM((6, nk, LANES), F32)
    readout = pltpu.VMEM((nk, LANES), F32)
    return pl.pallas_call(
        _rwkv_scan_kernel,
        grid=(t // steps,),
        in_specs=[seq] * 6 + [st],
        out_specs=[seq, st],
        out_shape=[jax.ShapeDtypeStruct((t, nb, d), F32), jax.ShapeDtypeStruct((nk, nk, LANES), F32)],
        scratch_shapes=[pltpu.VMEM((nk, nk, LANES), F32), operands, operands, readout, readout],
        compiler_params=_cparams("arbitrary"),
        name="rwkv_scan",
    )(r, w, k, v, a, b, s0)


def _rwkv_post_kernel(o_ref, r_ref, k_ref, v_ref, g_ref, x_ref, lnx_ref, rk_ref, ind_ref, indt_ref,
                      wout_ref, ln_ref, out_ref):
    ind = ind_ref[...]
    ind_t = indt_ref[...]
    inv_head = 1.0 / RWKV_HEAD
    o = o_ref[...]
    mu = _split_dot(_split_dot(o, ind) * inv_head, ind_t)
    oc = o - mu
    var = _split_dot(oc * oc, ind) * inv_head
    rstd = _split_dot(lax.rsqrt(var + LNX_EPS), ind_t)
    on = oc * rstd * lnx_ref[0:1, :] + lnx_ref[1:2, :]
    v = v_ref[...]
    bonus = _split_dot(_split_dot(r_ref[...] * k_ref[...] * rk_ref[...], ind), ind_t)
    on = on + bonus * v
    h = _dot((on * g_ref[...]).astype(BF16), wout_ref[...])
    out_ref[...] = _layernorm(DEEPNORM_ALPHA * x_ref[...] + h, ln_ref[0:1, :], ln_ref[1:2, :])


def _rwkv_post(o, r, k, v, g, x, p, ln, t, time_major):
    n, d = x.shape
    tm = _row_tile(t, (344, 256, 128, 64, 8)) if time_major else _row_tile(n, (384, 256, 128, 64, 8))
    row = pl.BlockSpec((tm, d), lambda i: (i, 0))
    seq_spec = _time_major_spec(tm, d, t // tm) if time_major else row
    weights = [p["lnx"], p["r_k"], p["ind"], p["ind_t"], p["w_out"], ln]
    return pl.pallas_call(
        _rwkv_post_kernel,
        grid=(n // tm,),
        in_specs=[seq_spec] * 4 + [row] * 2 + [_full(w.shape) for w in weights],
        out_specs=row,
        out_shape=jax.ShapeDtypeStruct((n, d), F32),
        compiler_params=_cparams("parallel"),
        name="rwkv_post",
    )(o, r, k, v, g, x, *weights)


def _proj_kernel(*refs, n_out):
    x_ref = refs[0]
    w_refs = refs[1:1 + n_out]
    out_refs = refs[1 + n_out:]
    x = x_ref[...].astype(BF16)
    for w_ref, out_ref in zip(w_refs, out_refs):
        out_ref[...] = _dot(x, w_ref[...])


def _proj(x, ws, name):
    n, d = x.shape
    tm = _row_tile(n, (384, 256, 128, 64, 8))
    return pl.pallas_call(
        functools.partial(_proj_kernel, n_out=len(ws)),
        grid=(n // tm,),
        in_specs=[pl.BlockSpec((tm, d), lambda i: (i, 0))] + [_full(w.shape) for w in ws],
        out_specs=[pl.BlockSpec((tm, w.shape[1]), lambda i: (i, 0)) for w in ws],
        out_shape=[jax.ShapeDtypeStruct((n, w.shape[1]), F32) for w in ws],
        compiler_params=_cparams("parallel"),
        name=name,
    )(x, *ws)


def _proj_ln_kernel(a_ref, x_ref, w_ref, ln_ref, out_ref):
    h = _dot(a_ref[...].astype(BF16), w_ref[...])
    out_ref[...] = _layernorm(DEEPNORM_ALPHA * x_ref[...] + h, ln_ref[0:1, :], ln_ref[1:2, :])


def _proj_ln(a, x, w, ln):
    n, d = x.shape
    tm = _row_tile(n, (384, 256, 128, 64, 8))
    row = pl.BlockSpec((tm, d), lambda i: (i, 0))
    return pl.pallas_call(
        _proj_ln_kernel,
        grid=(n // tm,),
        in_specs=[pl.BlockSpec((tm, a.shape[1]), lambda i: (i, 0)), row, _full(w.shape), _full(ln.shape)],
        out_specs=row,
        out_shape=jax.ShapeDtypeStruct((n, d), F32),
        compiler_params=_cparams("parallel"),
        name="proj_ln",
    )(a, x, w, ln)


MOE_TILE = 256


def _router_kernel(x_ref, wt_ref, bias_ref, tri_ref, eidx_ref, wts_ref, rank_ref, cnt_ref, run_ref):
    step = pl.program_id(0)

    @pl.when(step == 0)
    def _():
        run_ref[...] = jnp.zeros(run_ref.shape, F32)

    x = x_ref[...]
    x_hi = x.astype(BF16)
    x_lo = (x - x_hi.astype(F32)).astype(BF16)
    wt = wt_ref[...]
    wt_hi = wt.astype(BF16)
    wt_lo = (wt - wt_hi.astype(F32)).astype(BF16)
    logits = _dot_nt(wt_hi, x_hi) + _dot_nt(wt_hi, x_lo) + _dot_nt(wt_lo, x_hi)
    s = jax.nn.sigmoid(logits)
    sel = s + bias_ref[...]
    s_rows = [s[e:e + 1, :] for e in range(N_EXPERTS)]
    sel_rows = [sel[e:e + 1, :] for e in range(N_EXPERTS)]

    best_grp = None
    best_idx = None
    for gi in range(N_GROUPS):
        rows = sel_rows[gi * EXPERTS_PER_GROUP:(gi + 1) * EXPERTS_PER_GROUP]
        score = None
        for i in range(EXPERTS_PER_GROUP):
            for j in range(i + 1, EXPERTS_PER_GROUP):
                pair = rows[i] + rows[j]
                score = pair if score is None else jnp.maximum(score, pair)
        if best_grp is None:
            best_grp = score
            best_idx = jnp.zeros(score.shape, jnp.int32)
        else:
            better = score > best_grp
            best_grp = jnp.where(better, score, best_grp)
            best_idx = jnp.where(better, gi, best_idx)

    neg_inf = jnp.float32(-jnp.inf)
    masked = [jnp.where(best_idx == e // EXPERTS_PER_GROUP, sel_rows[e], neg_inf)
              for e in range(N_EXPERTS)]

    def argmax_rows(exclude):
        top = jnp.full(masked[0].shape, neg_inf)
        idx = jnp.full(masked[0].shape, -1, jnp.int32)
        val = jnp.zeros(masked[0].shape, F32)
        for e in range(N_EXPERTS):
            cand = masked[e] if exclude is None else jnp.where(exclude == e, neg_inf, masked[e])
            better = cand > top
            top = jnp.where(better, cand, top)
            idx = jnp.where(better, e, idx)
            val = jnp.where(better, s_rows[e], val)
        return idx, val

    i1, w1 = argmax_rows(None)
    i2, w2 = argmax_rows(i1)
    tot = w1 + w2
    eidx_ref[0:1, :] = i1
    eidx_ref[1:2, :] = i2
    wts_ref[0:1, :] = w1 / tot
    wts_ref[1:2, :] = w2 / tot

    e_iota = lax.broadcasted_iota(jnp.int32, logits.shape, 0)
    run = run_ref[...]
    for slot, idx in enumerate((i1, i2)):
        onehot = jnp.where(e_iota == idx, 1.0, 0.0)
        prefix = _dot(onehot.astype(BF16), tri_ref[...])
        rank = jnp.sum(onehot * (run[:, 0:1] + prefix - 1.0), axis=0, keepdims=True)
        rank_ref[slot:slot + 1, :] = rank.astype(jnp.int32)
        run = run + jnp.sum(onehot, axis=-1, keepdims=True)
    run_ref[...] = run
    cnt_ref[...] = run


def _router(x, router_wt, router_b):
    n, d = x.shape
    tm = _row_tile(n, (384, 256, 128))
    tri = (jnp.arange(tm)[:, None] <= jnp.arange(tm)[None, :]).astype(BF16)
    pair = pl.BlockSpec((TOP_K, tm), lambda i: (0, i))
    return pl.pallas_call(
        _router_kernel,
        grid=(n // tm,),
        in_specs=[pl.BlockSpec((tm, d), lambda i: (i, 0)), _full(router_wt.shape), _full(router_b.shape),
                  _full(tri.shape)],
        out_specs=[pair, pair, pair, _full((N_EXPERTS, LANES))],
        out_shape=[jax.ShapeDtypeStruct((TOP_K, n), jnp.int32), jax.ShapeDtypeStruct((TOP_K, n), F32),
                   jax.ShapeDtypeStruct((TOP_K, n), jnp.int32), jax.ShapeDtypeStruct((N_EXPERTS, LANES), F32)],
        scratch_shapes=[pltpu.VMEM((N_EXPERTS, LANES), F32)],
        compiler_params=_cparams("arbitrary"),
        name="router",
    )(x, router_wt, router_b, tri)


DMA_UNROLL = 8


def _dispatch_kernel(pos_ref, x_ref, init_hbm, xs_hbm, sem):
    del init_hbm
    tm = x_ref.shape[0]

    def row_copy(slot, r):
        return pltpu.make_async_copy(x_ref.at[pl.ds(r, 1)], xs_hbm.at[pl.ds(pos_ref[0, slot, r], 1)], sem)

    def start(r, c):
        for slot in range(TOP_K):
            row_copy(slot, r).start()
        return c

    def wait(r, c):
        for slot in range(TOP_K):
            row_copy(slot, r).wait()
        return c

    lax.fori_loop(0, tm, start, 0, unroll=DMA_UNROLL)
    lax.fori_loop(0, tm, wait, 0, unroll=DMA_UNROLL)


def _dispatch(x, pos_blocks, n_rows):
    n, d = x.shape
    nt, _, tm = pos_blocks.shape
    return pl.pallas_call(
        _dispatch_kernel,
        grid=(nt,),
        in_specs=[pl.BlockSpec((1, TOP_K, tm), lambda i: (i, 0, 0), memory_space=pltpu.SMEM),
                  pl.BlockSpec((tm, d), lambda i: (i, 0)),
                  pl.BlockSpec(memory_space=pl.ANY)],
        out_specs=pl.BlockSpec(memory_space=pl.ANY),
        out_shape=jax.ShapeDtypeStruct((n_rows, d), F32),
        scratch_shapes=[pltpu.SemaphoreType.DMA(())],
        input_output_aliases={2: 0},
        compiler_params=_cparams("arbitrary"),
        name="moe_dispatch",
    )(pos_blocks, x, jnp.zeros((n_rows, d), F32))


def _experts_kernel(tile_e_ref, n_used_ref, xs_ref, wg_ref, wu_ref, wd_ref, ys_ref):
    del tile_e_ref
    i = pl.program_id(0)

    @pl.when(i < n_used_ref[0])
    def _():
        x = xs_ref[...].astype(BF16)
        hg = _dot(x, wg_ref[0])
        hu = _dot(x, wu_ref[0])
        h = (hg * jax.nn.sigmoid(hg) * hu).astype(BF16)
        ys_ref[...] = _dot(h, wd_ref[0])

    @pl.when(i >= n_used_ref[0])
    def _():
        ys_ref[...] = jnp.zeros(ys_ref.shape, F32)


def _experts(xs, tile_e, n_used, wg, wu, wd):
    n_rows, d = xs.shape
    d_exp = wg.shape[2]
    tm = MOE_TILE

    def w_map(i, te, nu):
        return (te[i], 0, 0)

    return pl.pallas_call(
        _experts_kernel,
        grid_spec=pltpu.PrefetchScalarGridSpec(
            num_scalar_prefetch=2,
            grid=(n_rows // tm,),
            in_specs=[
                pl.BlockSpec((tm, d), lambda i, te, nu: (jnp.minimum(i, nu[0] - 1), 0)),
                pl.BlockSpec((1, d, d_exp), w_map),
                pl.BlockSpec((1, d, d_exp), w_map),
                pl.BlockSpec((1, d_exp, d), w_map),
            ],
            out_specs=pl.BlockSpec((tm, d), lambda i, te, nu: (i, 0)),
        ),
        out_shape=jax.ShapeDtypeStruct((n_rows, d), F32),
        compiler_params=_cparams("arbitrary"),
        name="moe_experts",
    )(tile_e, n_used, xs, wg, wu, wd)


def _combine_ln_kernel(pos_ref, wts_ref, x_ref, ln_ref, ys_hbm, out_ref, ybuf, sem):
    tm = x_ref.shape[0]

    def row_copy(slot, r):
        return pltpu.make_async_copy(ys_hbm.at[pl.ds(pos_ref[0, slot, r], 1)], ybuf.at[slot, pl.ds(r, 1)], sem)

    def start(r, c):
        for slot in range(TOP_K):
            row_copy(slot, r).start()
        return c

    def wait(r, c):
        for slot in range(TOP_K):
            row_copy(slot, r).wait()
        return c

    lax.fori_loop(0, tm, start, 0, unroll=DMA_UNROLL)
    lax.fori_loop(0, tm, wait, 0, unroll=DMA_UNROLL)
    y = wts_ref[:, 0:1] * ybuf[0] + wts_ref[:, 1:2] * ybuf[1]
    out_ref[...] = _layernorm(DEEPNORM_ALPHA * x_ref[...] + y, ln_ref[0:1, :], ln_ref[1:2, :])


def _combine_ln(ys, pos_blocks, wts_t, x, ln):
    n, d = x.shape
    nt, _, tm = pos_blocks.shape
    row = pl.BlockSpec((tm, d), lambda i: (i, 0))
    return pl.pallas_call(
        _combine_ln_kernel,
        grid=(nt,),
        in_specs=[pl.BlockSpec((1, TOP_K, tm), lambda i: (i, 0, 0), memory_space=pltpu.SMEM),
                  pl.BlockSpec((tm, TOP_K), lambda i: (i, 0)), row, _full(ln.shape),
                  pl.BlockSpec(memory_space=pl.ANY)],
        out_specs=row,
        out_shape=jax.ShapeDtypeStruct((n, d), F32),
        scratch_shapes=[pltpu.VMEM((TOP_K, tm, d), F32), pltpu.SemaphoreType.DMA(())],
        compiler_params=_cparams("arbitrary"),
        name="moe_combine_ln",
    )(pos_blocks, wts_t, x, ln, ys)


def _moe_ln(x, ln, router_wt, router_b, wg, wu, wd):
    n, d = x.shape
    eidx, wts, rank, cnt = _router(x, router_wt, router_b)

    n_tiles = pl.cdiv(TOP_K * n, MOE_TILE) + N_EXPERTS
    counts = cnt[:, 0].astype(jnp.int32)
    tiles_per_e = (counts + MOE_TILE - 1) // MOE_TILE
    tile_end = jnp.cumsum(tiles_per_e)
    tile_start = tile_end - tiles_per_e
    n_used = tile_end[-1:]
    experts = jnp.arange(N_EXPERTS, dtype=jnp.int32)
    base = jnp.sum(jnp.where(eidx[..., None] == experts, tile_start * MOE_TILE, 0), axis=-1)
    pos = base + rank
    tile_ids = jnp.minimum(jnp.arange(n_tiles, dtype=jnp.int32), n_used - 1)
    tile_e = jnp.sum((tile_end[None, :] <= tile_ids[:, None]).astype(jnp.int32), axis=1)
    tile_e = jnp.minimum(tile_e, N_EXPERTS - 1)

    tm = _row_tile(n, (384, 256, 128))
    pos_blocks = jnp.transpose(pos.reshape(TOP_K, n // tm, tm), (1, 0, 2))
    xs = _dispatch(x, pos_blocks, n_tiles * MOE_TILE)
    ys = _experts(xs, tile_e, n_used, wg, wu, wd)
    return _combine_ln(ys, pos_blocks, wts.T, x, ln)


def _lambda_value(lam_ref):
    lf = lam_ref[...]
    s01 = jnp.sum(lf[0:1, :] * lf[1:2, :], axis=-1, keepdims=True)
    s23 = jnp.sum(lf[2:3, :] * lf[3:4, :], axis=-1, keepdims=True)
    return jnp.exp(s01) - jnp.exp(s23) + LAMBDA_INIT


def _sub_ln(o, subln):
    o = o * lax.rsqrt(jnp.mean(o * o, axis=-1, keepdims=True) + SUBLN_EPS) * subln
    return o * (1.0 - LAMBDA_INIT)


def _split_maps(q):
    lane = lax.broadcasted_iota(jnp.int32, q.shape, 1)
    q0 = jnp.where(lane < DIFF_HEAD, q, 0.0).astype(BF16)
    q1 = jnp.where(lane >= DIFF_HEAD, q, 0.0).astype(BF16)
    return q0, q1


def _attn_prompt_kernel(lam_ref, subln_ref, q_ref, k_ref, v_ref, o_ref):
    qi = pl.program_id(2)
    tq = q_ref.shape[1]
    q0, q1 = _split_maps(q_ref[0] * (DIFF_HEAD ** -0.5))

    def chunk(j, carry, visible):
        start = pl.multiple_of(j * tq, SUBLANES)
        kc = k_ref[0, pl.ds(start, tq), :].astype(BF16)
        vc = v_ref[0, pl.ds(start, tq), :].astype(BF16)
        out = []
        for qm, (m, l, acc) in zip((q0, q1), carry):
            s = _dot_nt(qm, kc)
            if visible is not None:
                s = jnp.where(visible, s, -jnp.inf)
            m_new = jnp.maximum(m, jnp.max(s, axis=-1, keepdims=True))
            p = jnp.exp(s - m_new)
            scale = jnp.exp(m - m_new)
            l_new = scale * l + jnp.sum(p, axis=-1, keepdims=True)
            acc_new = scale * acc + _dot(p.astype(BF16), vc)
            out.append((m_new, l_new, acc_new))
        return tuple(out)

    init = (jnp.full((tq, 1), -jnp.inf, F32), jnp.zeros((tq, 1), F32), jnp.zeros((tq, 2 * DIFF_HEAD), F32))
    carry = lax.fori_loop(0, qi, functools.partial(chunk, visible=None), (init, init))
    causal = (lax.broadcasted_iota(jnp.int32, (tq, tq), 1) <= lax.broadcasted_iota(jnp.int32, (tq, tq), 0))
    (_, l0, a0), (_, l1, a1) = chunk(qi, carry, causal)
    o = a0 / l0 - _lambda_value(lam_ref) * (a1 / l1)
    o_ref[0] = _sub_ln(o, subln_ref[...])


def _attn_prompt(q, k, v, lam_vecs, subln):
    b, t, d = q.shape
    hw = 2 * DIFF_HEAD
    tq = _row_tile(t, (344, 256, 128, 64, 8))
    return pl.pallas_call(
        _attn_prompt_kernel,
        grid=(b, d // hw, t // tq),
        in_specs=[
            _full(lam_vecs.shape),
            _full(subln.shape),
            pl.BlockSpec((1, tq, hw), lambda bi, h, qi: (bi, qi, h)),
            pl.BlockSpec((1, t, hw), lambda bi, h, qi: (bi, 0, h)),
            pl.BlockSpec((1, t, hw), lambda bi, h, qi: (bi, 0, h)),
        ],
        out_specs=pl.BlockSpec((1, tq, hw), lambda bi, h, qi: (bi, qi, h)),
        out_shape=jax.ShapeDtypeStruct((b, t, d), F32),
        compiler_params=_cparams("parallel", "parallel", "arbitrary"),
        name="attn_prompt",
    )(lam_vecs, subln, q, k, v)


PAGES_PER_STEP = 4


def _attn_decode_kernel(pt_ref, lam_ref, subln_ref, kmask_ref, kfold_ref, vspread_ref, vmask_ref, q_ref, *refs):
    del pt_ref
    g = PAGES_PER_STEP
    kc_refs, vc_refs = refs[0:g], refs[g:2 * g]
    kn_ref, vn_ref, o_ref, qm_ref, qq_ref, m_ref, l_ref, acc_ref = refs[2 * g:]
    p = pl.program_id(1)
    tq = q_ref.shape[1]
    page, n_maps, _ = kc_refs[0].shape[1:]
    hw = 2 * DIFF_HEAD
    n_heads = n_maps // 2
    rows = 2 * tq

    @pl.when(p == 0)
    def _():
        m_ref[...] = jnp.full(m_ref.shape, -jnp.inf, F32)
        l_ref[...] = jnp.zeros(l_ref.shape, F32)
        acc_ref[...] = jnp.zeros(acc_ref.shape, F32)
        q = q_ref[0] * (DIFF_HEAD ** -0.5)
        qm_ref[...] = jnp.concatenate([q[:, m * DIFF_HEAD:(m + 1) * DIFF_HEAD] for m in range(n_maps)],
                                      axis=0).astype(BF16)
        for h in range(n_heads):
            q0, q1 = _split_maps(q[:, h * hw:(h + 1) * hw])
            qq_ref[h] = jnp.concatenate([q0, q1], axis=0)

    def update(s, pv_fn, visible):
        if visible is not None:
            s = [jnp.where(visible, sj, -jnp.inf) for sj in s]
        m_old = m_ref[...]
        m_new = jnp.maximum(m_old, jnp.max(functools.reduce(jnp.maximum, s), axis=-1, keepdims=True))
        pr = [jnp.exp(sj - m_new) for sj in s]
        scale = jnp.exp(m_old - m_new)
        l_ref[...] = scale * l_ref[...] + jnp.sum(functools.reduce(jnp.add, pr), axis=-1, keepdims=True)
        acc_ref[...] = scale * acc_ref[...] + pv_fn([pj.astype(BF16) for pj in pr])
        m_ref[...] = m_new

    def page_scores(kc_ref):
        kp = kc_ref[0].reshape(page * n_maps, DIFF_HEAD).astype(BF16)
        s_all = _dot_nt(qm_ref[...], kp) * kmask_ref[...]
        return _split_dot(s_all, kfold_ref[...])

    def page_pv(pr):
        out = None
        for pj, vc_ref in zip(pr, vc_refs):
            vp = vc_ref[0].reshape(page * n_heads, hw).astype(BF16)
            spread = (_dot(pj, vspread_ref[...]) * vmask_ref[...]).astype(BF16)
            d = _dot(spread, vp)
            out = d if out is None else out + d
        return out

    update([page_scores(r) for r in kc_refs], page_pv, None)

    @pl.when(p == pl.num_programs(1) - 1)
    def _():
        q_idx = lax.broadcasted_iota(jnp.int32, (n_heads * rows, page), 0) % tq
        k_idx = lax.broadcasted_iota(jnp.int32, (n_heads * rows, page), 1)
        pad = jnp.zeros((page - tq, hw), BF16)

        def head_block(ref, h):
            return jnp.concatenate([ref[0, :, h * hw:(h + 1) * hw].astype(BF16), pad], axis=0)

        s_new = jnp.concatenate([_dot_nt(qq_ref[h], head_block(kn_ref, h)) for h in range(n_heads)], axis=0)

        def new_pv(pr):
            return jnp.concatenate([_dot(pr[0][h * rows:(h + 1) * rows, :], head_block(vn_ref, h))
                                    for h in range(n_heads)], axis=0)

        update([s_new], new_pv, k_idx <= q_idx)
        lam = _lambda_value(lam_ref)
        on = acc_ref[...] / l_ref[...]
        for h in range(n_heads):
            o = on[h * rows:h * rows + tq, :] - lam * on[h * rows + tq:(h + 1) * rows, :]
            o_ref[0, :, h * hw:(h + 1) * hw] = _sub_ln(o, subln_ref[...])


def _attn_decode(q, cache_k, cache_v, page_table, k_new, v_new, lam_vecs, subln):
    db, tq, d = q.shape
    n_pages = page_table.shape[1]
    _, page, n_maps, _ = cache_k.shape
    hw = 2 * DIFF_HEAD
    n_heads = d // hw
    g = PAGES_PER_STEP
    assert tq % SUBLANES == 0 and tq <= page and page == hw and n_pages % g == 0
    assert cache_k.shape[2:] == (2 * n_heads, DIFF_HEAD) and cache_v.shape[1:] == (page, n_heads, hw)

    row_map = jnp.arange(n_maps * tq)[:, None] // tq
    k_col = jnp.arange(page * n_maps)[None, :]
    v_col = jnp.arange(page * n_heads)[None, :]
    keys = jnp.arange(page)
    kmask = (row_map == k_col % n_maps).astype(F32)
    kfold = (k_col.T // n_maps == keys[None, :]).astype(BF16)
    vspread = (keys[:, None] == v_col // n_heads).astype(BF16)
    vmask = (row_map // 2 == v_col % n_heads).astype(F32)
    consts = [lam_vecs, subln, kmask, kfold, vspread, vmask]

    def page_map(j):
        return lambda bi, p, pt: (pt[bi * n_pages + p * g + j], 0, 0, 0)

    def batch_map(bi, p, pt):
        return (bi, 0, 0)

    k_specs = [pl.BlockSpec((1,) + cache_k.shape[1:], page_map(j)) for j in range(g)]
    v_specs = [pl.BlockSpec((1,) + cache_v.shape[1:], page_map(j)) for j in range(g)]
    stat = pltpu.VMEM((n_maps * tq, hw), F32)
    return pl.pallas_call(
        _attn_decode_kernel,
        grid_spec=pltpu.PrefetchScalarGridSpec(
            num_scalar_prefetch=1,
            grid=(db, n_pages // g),
            in_specs=[
                *[pl.BlockSpec(c.shape, lambda bi, p, pt: (0, 0)) for c in consts],
                pl.BlockSpec((1, tq, d), batch_map),
                *k_specs,
                *v_specs,
                pl.BlockSpec((1, tq, d), batch_map),
                pl.BlockSpec((1, tq, d), batch_map),
            ],
            out_specs=pl.BlockSpec((1, tq, d), batch_map),
            scratch_shapes=[pltpu.VMEM((n_maps * tq, DIFF_HEAD), BF16),
                            pltpu.VMEM((n_heads, 2 * tq, hw), BF16), stat, stat, stat],
        ),
        out_shape=jax.ShapeDtypeStruct((db, tq, d), F32),
        compiler_params=_cparams("parallel", "arbitrary"),
        name="attn_decode",
    )(page_table.reshape(-1), *consts, q, *([cache_k] * g), *([cache_v] * g), k_new, v_new)


def _to_head_batch_lanes(z, b, seq, heads):
    z = z.reshape(b, seq, heads, RWKV_HEAD)
    return jnp.transpose(z, (1, 3, 2, 0)).reshape(seq, RWKV_HEAD, heads * b)


def _from_head_batch_lanes(z, b, seq, heads):
    z = z.reshape(seq, RWKV_HEAD, heads, b)
    return jnp.transpose(z, (3, 0, 2, 1)).reshape(b * seq, heads * RWKV_HEAD)


def _run(x, wkv_in, shift_in, past, pr):
    b, t, d = x.shape
    n = b * t
    heads = d // RWKV_HEAD
    xf = x.reshape(n, d)

    x_prev = jnp.concatenate([shift_in[:, None, :], x[:, :-1]], axis=1).reshape(n, d)
    time_major = b != LANES
    r, w, k, v, a, bb, g = _rwkv_pre(xf, x_prev, pr["rwkv"], b, t, time_major)
    if not time_major:
        scan_ops = [_to_head_batch_lanes(z, b, t, heads) for z in (r, w, k, v, a, bb)]
        o_l, s_fin = _rwkv_scan_native_state(*scan_ops, wkv_in.reshape(b, -1))
        o = _from_head_batch_lanes(o_l, b, t, heads)
        new_wkv = s_fin.reshape(wkv_in.shape)[:, None]
    else:
        s0 = jnp.transpose(wkv_in.reshape(b, heads // 2, 2, RWKV_HEAD, RWKV_HEAD), (4, 3, 2, 1, 0))
        o3, s_fin = _rwkv_scan(*[z.reshape(t, b, d) for z in (r, w, k, v, a, bb)],
                               s0.reshape(RWKV_HEAD, RWKV_HEAD, heads * b))
        o = o3.reshape(t, b * d)
        s_fin = s_fin.reshape(RWKV_HEAD, RWKV_HEAD, 2, heads // 2, b)
        new_wkv = jnp.transpose(s_fin, (4, 3, 2, 1, 0)).reshape(wkv_in.shape)[:, None]
    new_shift = x[:, -1][:, None]
    x1 = _rwkv_post(o, r, k, v, g, xf, pr["rwkv"], pr["post_ln"][0][0], t, time_major)
    x2 = _moe_ln(x1, pr["post_ln"][0][1], pr["router_wt"], pr["router_b"], *pr["moe"][0])

    k_new, v_new, q = _proj(x2, [pr["w_k_shared"], pr["w_v_shared"], pr["diff_w_q"]], "kv_q_proj")
    k3, v3, q3 = (z.reshape(b, t, d) for z in (k_new, v_new, q))
    if past is None:
        att = _attn_prompt(q3, k3, v3, pr["diff_lambda"], pr["diff_subln"])
    else:
        cache_k, cache_v, page_table = past
        att = _attn_decode(q3, cache_k, cache_v, page_table, k3, v3, pr["diff_lambda"], pr["diff_subln"])
    x3 = _proj_ln(att.reshape(n, d), x2, pr["diff_w_out"], pr["post_ln"][1][0])
    x4 = _moe_ln(x3, pr["post_ln"][1][1], pr["router_wt"], pr["router_b"], *pr["moe"][1])

    n_diff = d // (2 * DIFF_HEAD)
    return (x4.reshape(b, t, d), k_new.reshape(b, t, 2 * n_diff, DIFF_HEAD),
            v_new.reshape(b, t, n_diff, 2 * DIFF_HEAD), new_wkv, new_shift)


def kernel(x_prompt, x_sample, cache_k, cache_v, page_table, state_wkv, state_shift, meta_tokens,
           rwkv_mix, rwkv_w_rkv, rwkv_decay_w0, rwkv_decay_w1, rwkv_decay_w2, rwkv_a_w0, rwkv_a_w1,
           rwkv_a_w2, rwkv_g_w1, rwkv_g_w2, rwkv_kk_ka, rwkv_r_k, rwkv_lnx, rwkv_w_out, w_kv_shared,
           diff_w_q, diff_lambda, diff_subln, diff_w_out, router_w, router_b, moe_w_gate, moe_w_up,
           moe_w_down, post_ln):
    d = x_prompt.shape[-1]
    assert rwkv_mix.shape[0] == N_A_LAYERS and moe_w_gate.shape[0] == DEPTH
    ind, ind_t = _head_indicator(d, RWKV_HEAD)
    qk_width = diff_w_q.shape[-1]
    pr = {
        "rwkv": {
            "mix": rwkv_mix[0], "w_r": rwkv_w_rkv[0, 0].astype(BF16), "w_k": rwkv_w_rkv[0, 1].astype(BF16),
            "w_v": rwkv_w_rkv[0, 2].astype(BF16), "dw0": rwkv_decay_w0, "dw1": rwkv_decay_w1[0].astype(BF16),
            "dw2": rwkv_decay_w2[0].astype(BF16), "aw0": rwkv_a_w0, "aw1": rwkv_a_w1[0].astype(BF16),
            "aw2": rwkv_a_w2[0].astype(BF16), "gw1": rwkv_g_w1[0].astype(BF16),
            "gw2": rwkv_g_w2[0].astype(BF16), "kk_ka": rwkv_kk_ka[0], "r_k": rwkv_r_k[0].reshape(1, d),
            "lnx": rwkv_lnx[0], "w_out": rwkv_w_out[0].astype(BF16), "ind": ind, "ind_t": ind_t,
        },
        "post_ln": post_ln,
        "router_wt": router_w.T,
        "router_b": router_b.reshape(N_EXPERTS, 1),
        "moe": [(moe_w_gate[l].astype(BF16), moe_w_up[l].astype(BF16), moe_w_down[l].astype(BF16))
                for l in range(DEPTH)],
        "w_k_shared": w_kv_shared[:, :qk_width].astype(BF16),
        "w_v_shared": w_kv_shared[:, qk_width:].astype(BF16),
        "diff_w_q": diff_w_q[0].astype(BF16),
        "diff_lambda": diff_lambda[0],
        "diff_subln": diff_subln[0].reshape(1, -1),
        "diff_w_out": diff_w_out[0].astype(BF16),
    }

    bp = x_prompt.shape[0]
    n_meta = meta_tokens.shape[0]
    meta = jnp.broadcast_to(meta_tokens[None].astype(x_prompt.dtype), (bp, n_meta, d))
    xp = jnp.concatenate([meta, x_prompt], axis=1)
    wkv0 = jnp.zeros((bp,) + state_wkv.shape[2:], x_prompt.dtype)
    shift0 = jnp.zeros((bp, d), x_prompt.dtype)
    yp, k_p, v_p, wkv_p, shift_p = _run(xp, wkv0, shift0, None, pr)

    past = (cache_k, cache_v, page_table)
    ys, k_s, v_s, wkv_s, shift_s = _run(x_sample, state_wkv[:, 0], state_shift[:, 0], past, pr)
    return (yp[:, n_meta:], ys, k_p, v_p, wkv_p, shift_p, k_s, v_s, wkv_s, shift_s)
```

```python
import functools
import math

import jax
import jax.numpy as jnp
from jax import lax
from jax.experimental import pallas as pl
from jax.experimental.pallas import tpu as pltpu

F32 = jnp.float32
BF16 = jnp.bfloat16

DEPTH = 2
N_A_LAYERS = 1
RWKV_HEAD = 64
DIFF_HEAD = 64
LNX_EPS = 64e-5
SUBLN_EPS = 1e-5
LN_EPS = 1e-5
N_EXPERTS = 16
N_GROUPS = 4
EXPERTS_PER_GROUP = N_EXPERTS // N_GROUPS
TOP_K = 2
DEEPNORM_ALPHA = (2 * DEPTH) ** 0.25
LAMBDA_INIT = 0.8 - 0.6 * math.exp(-0.3 * 1)

LANES = 128
SUBLANES = 8
VMEM_LIMIT_BYTES = 56 * 1024 * 1024


def _cparams(*sem):
    return pltpu.CompilerParams(dimension_semantics=sem, vmem_limit_bytes=VMEM_LIMIT_BYTES)


def _row_tile(n, candidates):
    for c in candidates:
        if n % c == 0:
            return c
    raise ValueError(f"no row tile for {n} in {candidates}")


def _full(shape):
    zeros = (0,) * len(shape)
    return pl.BlockSpec(shape, lambda *_: zeros)


def _dot(a, b):
    return jnp.dot(a, b, preferred_element_type=F32)


def _dot_nt(a, b):
    return lax.dot_general(a, b, (((1,), (1,)), ((), ())), preferred_element_type=F32)


def _split_dot(x, w_bf16):
    hi = x.astype(BF16)
    lo = (x - hi.astype(F32)).astype(BF16)
    return _dot(hi, w_bf16) + _dot(lo, w_bf16)


def _layernorm(z, g, b):
    mu = jnp.mean(z, axis=-1, keepdims=True)
    zc = z - mu
    var = jnp.mean(zc * zc, axis=-1, keepdims=True)
    return zc * lax.rsqrt(var + LN_EPS) * g + b


def _head_indicator(d, head):
    n_heads = d // head
    assert n_heads <= LANES
    ind = (jnp.arange(d)[:, None] // head == jnp.arange(LANES)[None, :]).astype(BF16)
    return ind, ind.T


def _rwkv_pre_kernel(x_ref, xp_ref, mix_ref, wr_ref, wk_ref, wv_ref, dw0_ref, dw1_ref, dw2_ref,
                     aw0_ref, aw1_ref, aw2_ref, gw1_ref, gw2_ref, kkka_ref, ind_ref, indt_ref,
                     r_ref, w_ref, k_ref, v_ref, a_ref, b_ref, g_ref):
    x = x_ref[...]
    xx = xp_ref[...] - x

    def mixed(i):
        return (x + xx * mix_ref[i:i + 1, :]).astype(BF16)

    r = _dot(mixed(0), wr_ref[...])
    k = _dot(mixed(2), wk_ref[...])
    v = _dot(mixed(3), wv_ref[...])
    lw = jnp.tanh(_dot(mixed(1), dw1_ref[...])).astype(BF16)
    z = -(dw0_ref[...] + _dot(lw, dw2_ref[...]))
    softplus = jnp.maximum(z, 0.0) + jnp.log(1.0 + jnp.exp(-jnp.abs(z)))
    w_log = -softplus - 0.5
    decay = jnp.exp(-jnp.exp(w_log))
    la = _dot(mixed(4), aw1_ref[...]).astype(BF16)
    a = jax.nn.sigmoid(aw0_ref[...] + _dot(la, aw2_ref[...]))
    lg = jax.nn.sigmoid(_dot(mixed(5), gw1_ref[...])).astype(BF16)
    g = _dot(lg, gw2_ref[...])

    kk = k * kkka_ref[0:1, :]
    ss = _split_dot(_split_dot(kk * kk, ind_ref[...]), indt_ref[...])
    kk = kk / jnp.maximum(jnp.sqrt(ss), 1e-12)
    k = k * (1.0 + (a - 1.0) * kkka_ref[1:2, :])

    r_ref[...] = r
    w_ref[...] = decay
    k_ref[...] = k
    v_ref[...] = v
    a_ref[...] = -kk
    b_ref[...] = kk * a
    g_ref[...] = g


def _time_major_spec(tm, d, tiles_per_seq):
    return pl.BlockSpec((tm, d), lambda i: (i % tiles_per_seq, i // tiles_per_seq))


def _rwkv_pre(x, x_prev, p, b, t, time_major):
    n, d = x.shape
    tm = _row_tile(t, (344, 256, 128, 64, 8)) if time_major else _row_tile(n, (192, 128, 64, 8))
    row = pl.BlockSpec((tm, d), lambda i: (i, 0))
    seq_spec = _time_major_spec(tm, d, t // tm) if time_major else row
    seq_shape = (t, b * d) if time_major else (n, d)
    weights = [p["mix"], p["w_r"], p["w_k"], p["w_v"], p["dw0"], p["dw1"], p["dw2"], p["aw0"],
               p["aw1"], p["aw2"], p["gw1"], p["gw2"], p["kk_ka"], p["ind"], p["ind_t"]]
    return pl.pallas_call(
        _rwkv_pre_kernel,
        grid=(n // tm,),
        in_specs=[row, row] + [_full(w.shape) for w in weights],
        out_specs=[seq_spec] * 6 + [row],
        out_shape=[jax.ShapeDtypeStruct(seq_shape, F32)] * 6 + [jax.ShapeDtypeStruct((n, d), F32)],
        compiler_params=_cparams("parallel"),
        name="rwkv_pre",
    )(x, x_prev, *weights)


SCAN_VALUE_SPLIT = 2


def _scan_groups(s_ref):
    nv = s_ref.shape[1] // SCAN_VALUE_SPLIT
    return nv, range(0, s_ref.shape[1], nv)


def _state_times(s_ref, vec_ref):
    nv, groups = _scan_groups(s_ref)
    out = []
    for lo in groups:
        acc = [jnp.zeros((nv, LANES), F32), jnp.zeros((nv, LANES), F32)]
        for j in range(s_ref.shape[0]):
            acc[j % 2] = acc[j % 2] + s_ref[j, lo:lo + nv, :] * vec_ref[pl.ds(j, 1), :]
        out.append(acc[0] + acc[1])
    return tuple(out)


def _scan_step(s_ref, r_ref, w_ref, k_ref, v_ref, b_ref, a_next_ref, o_ref, sa_all):
    nv, groups = _scan_groups(s_ref)
    sa_next = []
    for g, lo in enumerate(groups):
        vt = v_ref[lo:lo + nv, :]
        sa = sa_all[g]
        ot = [jnp.zeros_like(vt), jnp.zeros_like(vt)]
        sn = [jnp.zeros_like(vt), jnp.zeros_like(vt)]
        for j in range(s_ref.shape[0]):
            s = (s_ref[j, lo:lo + nv, :] * w_ref[pl.ds(j, 1), :] + sa * b_ref[pl.ds(j, 1), :]
                 + vt * k_ref[pl.ds(j, 1), :])
            s_ref[j, lo:lo + nv, :] = s
            ot[j % 2] = ot[j % 2] + s * r_ref[pl.ds(j, 1), :]
            sn[j % 2] = sn[j % 2] + s * a_next_ref[pl.ds(j, 1), :]
        o_ref[lo:lo + nv, :] = ot[0] + ot[1]
        sa_next.append(sn[0] + sn[1])
    return tuple(sa_next)


def _rwkv_scan_kernel(r_ref, w_ref, k_ref, v_ref, a_ref, b_ref, s0_ref, o_ref, sout_ref, s_ref,
                      buf_a, buf_b, obuf_a, obuf_b):
    tc = pl.program_id(0)
    steps, nb, d = r_ref.shape
    n_chunks = d // LANES
    half = LANES // 2
    srcs = (r_ref, w_ref, k_ref, v_ref, a_ref, b_ref)

    @pl.when(tc == 0)
    def _():
        s_ref[...] = s0_ref[...]

    def to_lanes(t, buf):
        for i_op, src in enumerate(srcs):
            x = src[t]
            rows = jnp.concatenate([x[:, i * LANES:(i + 1) * LANES] for i in range(n_chunks)], axis=0)
            cols = rows.T
            buf[i_op] = jnp.concatenate([cols[:half], cols[half:]], axis=1)

    def from_lanes(obuf, t):
        y = obuf[...]
        rows = jnp.concatenate([y[:, :half], y[:, half:]], axis=0).T
        for i in range(n_chunks):
            o_ref[t, :, i * LANES:(i + 1) * LANES] = rows[i * nb:(i + 1) * nb, :]

    def step(cur, nxt, obuf, sa):
        r_b, w_b, k_b, v_b, _, b_b = (cur.at[i] for i in range(6))
        return _scan_step(s_ref, r_b, w_b, k_b, v_b, b_b, nxt.at[4], obuf, sa)

    def pair(i, sa):
        t = 2 * i
        to_lanes(t + 1, buf_b)
        from_lanes(obuf_b, jnp.maximum(t - 1, 0))
        sa = step(buf_a, buf_b, obuf_a, sa)
        to_lanes(jnp.minimum(t + 2, steps - 1), buf_a)
        from_lanes(obuf_a, t)
        return step(buf_b, buf_a, obuf_b, sa)

    to_lanes(0, buf_a)
    obuf_b[...] = jnp.zeros(obuf_b.shape, F32)
    lax.fori_loop(0, steps // 2, pair, _state_times(s_ref, buf_a.at[4]))
    from_lanes(obuf_b, steps - 1)

    @pl.when(tc == pl.num_programs(0) - 1)
    def _():
        sout_ref[...] = s_ref[...]


def _rwkv_scan_native_state_kernel(r_ref, w_ref, k_ref, v_ref, a_ref, b_ref, s0_ref, o_ref, sout_ref,
                                   s_ref, tmp_ref):
    steps = r_ref.shape[0]
    nk = s_ref.shape[0]
    tmp_ref[...] = s0_ref[...].T
    for j in range(nk):
        s_ref[j] = tmp_ref[pl.ds(j, nk, stride=nk), :]

    def step(t, sa):
        return _scan_step(s_ref, r_ref.at[t], w_ref.at[t], k_ref.at[t], v_ref.at[t], b_ref.at[t],
                          a_ref.at[jnp.minimum(t + 1, steps - 1)], o_ref.at[t], sa)

    lax.fori_loop(0, steps, step, _state_times(s_ref, a_ref.at[0]))
    for j in range(nk):
        tmp_ref[pl.ds(j, nk, stride=nk), :] = s_ref[j]
    sout_ref[...] = tmp_ref[...].T


def _rwkv_scan_native_state(r, w, k, v, a, b, state):
    t, nk, l = r.shape
    nb, width = state.shape
    assert nb == LANES and width == (l // LANES) * nk * nk
    seq = pl.BlockSpec((t, nk, LANES), lambda h: (0, 0, h))
    st = pl.BlockSpec((LANES, nk * nk), lambda h: (0, h))
    return pl.pallas_call(
        _rwkv_scan_native_state_kernel,
        grid=(l // LANES,),
        in_specs=[seq] * 6 + [st],
        out_specs=[seq, st],
        out_shape=[jax.ShapeDtypeStruct((t, nk, l), F32), jax.ShapeDtypeStruct(state.shape, F32)],
        scratch_shapes=[pltpu.VMEM((nk, nk, LANES), F32), pltpu.VMEM((nk * nk, LANES), F32)],
        compiler_params=_cparams("parallel"),
        name="rwkv_scan_native_state",
    )(r, w, k, v, a, b, state)


def _rwkv_scan(r, w, k, v, a, b, s0):
    t, nb, d = r.shape
    nk = RWKV_HEAD
    assert nb * (d // LANES) == LANES // 2 and d // nk == 2 * (d // LANES)
    steps = _row_tile(t, (48, 16, 8, 4, 2))
    seq = pl.BlockSpec((steps, nb, d), lambda c: (c, 0, 0))
    st = pl.BlockSpec((nk, nk, LANES), lambda c: (0, 0, 0))
    operands = pltpu.VMEM((6, nk, LANES), F32)
    readout = pltpu.VMEM((nk, LANES), F32)
    return pl.pallas_call(
        _rwkv_scan_kernel,
        grid=(t // steps,),
        in_specs=[seq] * 6 + [st],
        out_specs=[seq, st],
        out_shape=[jax.ShapeDtypeStruct((t, nb, d), F32), jax.ShapeDtypeStruct((nk, nk, LANES), F32)],
        scratch_shapes=[pltpu.VMEM((nk, nk, LANES), F32), operands, operands, readout, readout],
        compiler_params=_cparams("arbitrary"),
        name="rwkv_scan",
    )(r, w, k, v, a, b, s0)


def _rwkv_post_kernel(o_ref, r_ref, k_ref, v_ref, g_ref, x_ref, lnx_ref, rk_ref, ind_ref, indt_ref,
                      wout_ref, ln_ref, out_ref):
    ind = ind_ref[...]
    ind_t = indt_ref[...]
    inv_head = 1.0 / RWKV_HEAD
    o = o_ref[...]
    mu = _split_dot(_split_dot(o, ind) * inv_head, ind_t)
    oc = o - mu
    var = _split_dot(oc * oc, ind) * inv_head
    rstd = _split_dot(lax.rsqrt(var + LNX_EPS), ind_t)
    on = oc * rstd * lnx_ref[0:1, :] + lnx_ref[1:2, :]
    v = v_ref[...]
    bonus = _split_dot(_split_dot(r_ref[...] * k_ref[...] * rk_ref[...], ind), ind_t)
    on = on + bonus * v
    h = _dot((on * g_ref[...]).astype(BF16), wout_ref[...])
    out_ref[...] = _layernorm(DEEPNORM_ALPHA * x_ref[...] + h, ln_ref[0:1, :], ln_ref[1:2, :])


def _rwkv_post(o, r, k, v, g, x, p, ln, t, time_major):
    n, d = x.shape
    tm = _row_tile(t, (344, 256, 128, 64, 8)) if time_major else _row_tile(n, (384, 256, 128, 64, 8))
    row = pl.BlockSpec((tm, d), lambda i: (i, 0))
    seq_spec = _time_major_spec(tm, d, t // tm) if time_major else row
    weights = [p["lnx"], p["r_k"], p["ind"], p["ind_t"], p["w_out"], ln]
    return pl.pallas_call(
        _rwkv_post_kernel,
        grid=(n // tm,),
        in_specs=[seq_spec] * 4 + [row] * 2 + [_full(w.shape) for w in weights],
        out_specs=row,
        out_shape=jax.ShapeDtypeStruct((n, d), F32),
        compiler_params=_cparams("parallel"),
        name="rwkv_post",
    )(o, r, k, v, g, x, *weights)


def _proj_kernel(*refs, n_out):
    x_ref = refs[0]
    w_refs = refs[1:1 + n_out]
    out_refs = refs[1 + n_out:]
    x = x_ref[...].astype(BF16)
    for w_ref, out_ref in zip(w_refs, out_refs):
        out_ref[...] = _dot(x, w_ref[...])


def _proj(x, ws, name):
    n, d = x.shape
    tm = _row_tile(n, (384, 256, 128, 64, 8))
    return pl.pallas_call(
        functools.partial(_proj_kernel, n_out=len(ws)),
        grid=(n // tm,),
        in_specs=[pl.BlockSpec((tm, d), lambda i: (i, 0))] + [_full(w.shape) for w in ws],
        out_specs=[pl.BlockSpec((tm, w.shape[1]), lambda i: (i, 0)) for w in ws],
        out_shape=[jax.ShapeDtypeStruct((n, w.shape[1]), F32) for w in ws],
        compiler_params=_cparams("parallel"),
        name=name,
    )(x, *ws)


def _proj_ln_kernel(a_ref, x_ref, w_ref, ln_ref, out_ref):
    h = _dot(a_ref[...].astype(BF16), w_ref[...])
    out_ref[...] = _layernorm(DEEPNORM_ALPHA * x_ref[...] + h, ln_ref[0:1, :], ln_ref[1:2, :])


def _proj_ln(a, x, w, ln):
    n, d = x.shape
    tm = _row_tile(n, (384, 256, 128, 64, 8))
    row = pl.BlockSpec((tm, d), lambda i: (i, 0))
    return pl.pallas_call(
        _proj_ln_kernel,
        grid=(n // tm,),
        in_specs=[pl.BlockSpec((tm, a.shape[1]), lambda i: (i, 0)), row, _full(w.shape), _full(ln.shape)],
        out_specs=row,
        out_shape=jax.ShapeDtypeStruct((n, d), F32),
        compiler_params=_cparams("parallel"),
        name="proj_ln",
    )(a, x, w, ln)


MOE_TILE = 256


def _router_kernel(x_ref, wt_ref, bias_ref, tri_ref, eidx_ref, wts_ref, rank_ref, cnt_ref, run_ref):
    step = pl.program_id(0)

    @pl.when(step == 0)
    def _():
        run_ref[...] = jnp.zeros(run_ref.shape, F32)

    x = x_ref[...]
    x_hi = x.astype(BF16)
    x_lo = (x - x_hi.astype(F32)).astype(BF16)
    wt = wt_ref[...]
    wt_hi = wt.astype(BF16)
    wt_lo = (wt - wt_hi.astype(F32)).astype(BF16)
    logits = _dot_nt(wt_hi, x_hi) + _dot_nt(wt_hi, x_lo) + _dot_nt(wt_lo, x_hi)
    s = jax.nn.sigmoid(logits)
    sel = s + bias_ref[...]
    s_rows = [s[e:e + 1, :] for e in range(N_EXPERTS)]
    sel_rows = [sel[e:e + 1, :] for e in range(N_EXPERTS)]

    best_grp = None
    best_idx = None
    for gi in range(N_GROUPS):
        rows = sel_rows[gi * EXPERTS_PER_GROUP:(gi + 1) * EXPERTS_PER_GROUP]
        score = None
        for i in range(EXPERTS_PER_GROUP):
            for j in range(i + 1, EXPERTS_PER_GROUP):
                pair = rows[i] + rows[j]
                score = pair if score is None else jnp.maximum(score, pair)
        if best_grp is None:
            best_grp = score
            best_idx = jnp.zeros(score.shape, jnp.int32)
        else:
            better = score > best_grp
            best_grp = jnp.where(better, score, best_grp)
            best_idx = jnp.where(better, gi, best_idx)

    neg_inf = jnp.float32(-jnp.inf)
    masked = [jnp.where(best_idx == e // EXPERTS_PER_GROUP, sel_rows[e], neg_inf)
              for e in range(N_EXPERTS)]

    def argmax_rows(exclude):
        top = jnp.full(masked[0].shape, neg_inf)
        idx = jnp.full(masked[0].shape, -1, jnp.int32)
        val = jnp.zeros(masked[0].shape, F32)
        for e in range(N_EXPERTS):
            cand = masked[e] if exclude is None else jnp.where(exclude == e, neg_inf, masked[e])
            better = cand > top
            top = jnp.where(better, cand, top)
            idx = jnp.where(better, e, idx)
            val = jnp.where(better, s_rows[e], val)
        return idx, val

    i1, w1 = argmax_rows(None)
    i2, w2 = argmax_rows(i1)
    tot = w1 + w2
    eidx_ref[0:1, :] = i1
    eidx_ref[1:2, :] = i2
    wts_ref[0:1, :] = w1 / tot
    wts_ref[1:2, :] = w2 / tot

    e_iota = lax.broadcasted_iota(jnp.int32, logits.shape, 0)
    run = run_ref[...]
    for slot, idx in enumerate((i1, i2)):
        onehot = jnp.where(e_iota == idx, 1.0, 0.0)
        prefix = _dot(onehot.astype(BF16), tri_ref[...])
        rank = jnp.sum(onehot * (run[:, 0:1] + prefix - 1.0), axis=0, keepdims=True)
        rank_ref[slot:slot + 1, :] = rank.astype(jnp.int32)
        run = run + jnp.sum(onehot, axis=-1, keepdims=True)
    run_ref[...] = run
    cnt_ref[...] = run


def _router(x, router_wt, router_b):
    n, d = x.shape
    tm = _row_tile(n, (384, 256, 128))
    tri = (jnp.arange(tm)[:, None] <= jnp.arange(tm)[None, :]).astype(BF16)
    pair = pl.BlockSpec((TOP_K, tm), lambda i: (0, i))
    return pl.pallas_call(
        _router_kernel,
        grid=(n // tm,),
        in_specs=[pl.BlockSpec((tm, d), lambda i: (i, 0)), _full(router_wt.shape), _full(router_b.shape),
                  _full(tri.shape)],
        out_specs=[pair, pair, pair, _full((N_EXPERTS, LANES))],
        out_shape=[jax.ShapeDtypeStruct((TOP_K, n), jnp.int32), jax.ShapeDtypeStruct((TOP_K, n), F32),
                   jax.ShapeDtypeStruct((TOP_K, n), jnp.int32), jax.ShapeDtypeStruct((N_EXPERTS, LANES), F32)],
        scratch_shapes=[pltpu.VMEM((N_EXPERTS, LANES), F32)],
        compiler_params=_cparams("arbitrary"),
        name="router",
    )(x, router_wt, router_b, tri)


DMA_UNROLL = 8


def _dispatch_kernel(pos_ref, x_ref, init_hbm, xs_hbm, sem):
    del init_hbm
    tm = x_ref.shape[0]

    def row_copy(slot, r):
        return pltpu.make_async_copy(x_ref.at[pl.ds(r, 1)], xs_hbm.at[pl.ds(pos_ref[0, slot, r], 1)], sem)

    def start(r, c):
        for slot in range(TOP_K):
            row_copy(slot, r).start()
        return c

    def wait(r, c):
        for slot in range(TOP_K):
            row_copy(slot, r).wait()
        return c

    lax.fori_loop(0, tm, start, 0, unroll=DMA_UNROLL)
    lax.fori_loop(0, tm, wait, 0, unroll=DMA_UNROLL)


def _dispatch(x, pos_blocks, n_rows):
    n, d = x.shape
    nt, _, tm = pos_blocks.shape
    return pl.pallas_call(
        _dispatch_kernel,
        grid=(nt,),
        in_specs=[pl.BlockSpec((1, TOP_K, tm), lambda i: (i, 0, 0), memory_space=pltpu.SMEM),
                  pl.BlockSpec((tm, d), lambda i: (i, 0)),
                  pl.BlockSpec(memory_space=pl.ANY)],
        out_specs=pl.BlockSpec(memory_space=pl.ANY),
        out_shape=jax.ShapeDtypeStruct((n_rows, d), F32),
        scratch_shapes=[pltpu.SemaphoreType.DMA(())],
        input_output_aliases={2: 0},
        compiler_params=_cparams("arbitrary"),
        name="moe_dispatch",
    )(pos_blocks, x, jnp.zeros((n_rows, d), F32))


def _experts_kernel(tile_e_ref, n_used_ref, xs_ref, wg_ref, wu_ref, wd_ref, ys_ref):
    del tile_e_ref
    i = pl.program_id(0)

    @pl.when(i < n_used_ref[0])
    def _():
        x = xs_ref[...].astype(BF16)
        hg = _dot(x, wg_ref[0])
        hu = _dot(x, wu_ref[0])
        h = (hg * jax.nn.sigmoid(hg) * hu).astype(BF16)
        ys_ref[...] = _dot(h, wd_ref[0])

    @pl.when(i >= n_used_ref[0])
    def _():
        ys_ref[...] = jnp.zeros(ys_ref.shape, F32)


def _experts(xs, tile_e, n_used, wg, wu, wd):
    n_rows, d = xs.shape
    d_exp = wg.shape[2]
    tm = MOE_TILE

    def w_map(i, te, nu):
        return (te[i], 0, 0)

    return pl.pallas_call(
        _experts_kernel,
        grid_spec=pltpu.PrefetchScalarGridSpec(
            num_scalar_prefetch=2,
            grid=(n_rows // tm,),
            in_specs=[
                pl.BlockSpec((tm, d), lambda i, te, nu: (jnp.minimum(i, nu[0] - 1), 0)),
                pl.BlockSpec((1, d, d_exp), w_map),
                pl.BlockSpec((1, d, d_exp), w_map),
                pl.BlockSpec((1, d_exp, d), w_map),
            ],
            out_specs=pl.BlockSpec((tm, d), lambda i, te, nu: (i, 0)),
        ),
        out_shape=jax.ShapeDtypeStruct((n_rows, d), F32),
        compiler_params=_cparams("arbitrary"),
        name="moe_experts",
    )(tile_e, n_used, xs, wg, wu, wd)


def _combine_ln_kernel(pos_ref, wts_ref, x_ref, ln_ref, ys_hbm, out_ref, ybuf, sem):
    tm = x_ref.shape[0]

    def row_copy(slot, r):
        return pltpu.make_async_copy(ys_hbm.at[pl.ds(pos_ref[0, slot, r], 1)], ybuf.at[slot, pl.ds(r, 1)], sem)

    def start(r, c):
        for slot in range(TOP_K):
            row_copy(slot, r).start()
        return c

    def wait(r, c):
        for slot in range(TOP_K):
            row_copy(slot, r).wait()
        return c

    lax.fori_loop(0, tm, start, 0, unroll=DMA_UNROLL)
    lax.fori_loop(0, tm, wait, 0, unroll=DMA_UNROLL)
    y = wts_ref[:, 0:1] * ybuf[0] + wts_ref[:, 1:2] * ybuf[1]
    out_ref[...] = _layernorm(DEEPNORM_ALPHA * x_ref[...] + y, ln_ref[0:1, :], ln_ref[1:2, :])


def _combine_ln(ys, pos_blocks, wts_t, x, ln):
    n, d = x.shape
    nt, _, tm = pos_blocks.shape
    row = pl.BlockSpec((tm, d), lambda i: (i, 0))
    return pl.pallas_call(
        _combine_ln_kernel,
        grid=(nt,),
        in_specs=[pl.BlockSpec((1, TOP_K, tm), lambda i: (i, 0, 0), memory_space=pltpu.SMEM),
                  pl.BlockSpec((tm, TOP_K), lambda i: (i, 0)), row, _full(ln.shape),
                  pl.BlockSpec(memory_space=pl.ANY)],
        out_specs=row,
        out_shape=jax.ShapeDtypeStruct((n, d), F32),
        scratch_shapes=[pltpu.VMEM((TOP_K, tm, d), F32), pltpu.SemaphoreType.DMA(())],
        compiler_params=_cparams("arbitrary"),
        name="moe_combine_ln",
    )(pos_blocks, wts_t, x, ln, ys)


def _moe_ln(x, ln, router_wt, router_b, wg, wu, wd):
    n, d = x.shape
    eidx, wts, rank, cnt = _router(x, router_wt, router_b)

    n_tiles = pl.cdiv(TOP_K * n, MOE_TILE) + N_EXPERTS
    counts = cnt[:, 0].astype(jnp.int32)
    tiles_per_e = (counts + MOE_TILE - 1) // MOE_TILE
    tile_end = jnp.cumsum(tiles_per_e)
    tile_start = tile_end - tiles_per_e
    n_used = tile_end[-1:]
    experts = jnp.arange(N_EXPERTS, dtype=jnp.int32)
    base = jnp.sum(jnp.where(eidx[..., None] == experts, tile_start * MOE_TILE, 0), axis=-1)
    pos = base + rank
    tile_ids = jnp.minimum(jnp.arange(n_tiles, dtype=jnp.int32), n_used - 1)
    tile_e = jnp.sum((tile_end[None, :] <= tile_ids[:, None]).astype(jnp.int32), axis=1)
    tile_e = jnp.minimum(tile_e, N_EXPERTS - 1)

    tm = _row_tile(n, (384, 256, 128))
    pos_blocks = jnp.transpose(pos.reshape(TOP_K, n // tm, tm), (1, 0, 2))
    xs = _dispatch(x, pos_blocks, n_tiles * MOE_TILE)
    ys = _experts(xs, tile_e, n_used, wg, wu, wd)
    return _combine_ln(ys, pos_blocks, wts.T, x, ln)


def _lambda_value(lam_ref):
    lf = lam_ref[...]
    s01 = jnp.sum(lf[0:1, :] * lf[1:2, :], axis=-1, keepdims=True)
    s23 = jnp.sum(lf[2:3, :] * lf[3:4, :], axis=-1, keepdims=True)
    return jnp.exp(s01) - jnp.exp(s23) + LAMBDA_INIT


def _sub_ln(o, subln):
    o = o * lax.rsqrt(jnp.mean(o * o, axis=-1, keepdims=True) + SUBLN_EPS) * subln
    return o * (1.0 - LAMBDA_INIT)


def _split_maps(q):
    lane = lax.broadcasted_iota(jnp.int32, q.shape, 1)
    q0 = jnp.where(lane < DIFF_HEAD, q, 0.0).astype(BF16)
    q1 = jnp.where(lane >= DIFF_HEAD, q, 0.0).astype(BF16)
    return q0, q1


def _attn_prompt_kernel(lam_ref, subln_ref, q_ref, k_ref, v_ref, o_ref):
    qi = pl.program_id(2)
    tq = q_ref.shape[1]
    q0, q1 = _split_maps(q_ref[0] * (DIFF_HEAD ** -0.5))

    def chunk(j, carry, visible):
        start = pl.multiple_of(j * tq, SUBLANES)
        kc = k_ref[0, pl.ds(start, tq), :].astype(BF16)
        vc = v_ref[0, pl.ds(start, tq), :].astype(BF16)
        out = []
        for qm, (m, l, acc) in zip((q0, q1), carry):
            s = _dot_nt(qm, kc)
            if visible is not None:
                s = jnp.where(visible, s, -jnp.inf)
            m_new = jnp.maximum(m, jnp.max(s, axis=-1, keepdims=True))
            p = jnp.exp(s - m_new)
            scale = jnp.exp(m - m_new)
            l_new = scale * l + jnp.sum(p, axis=-1, keepdims=True)
            acc_new = scale * acc + _dot(p.astype(BF16), vc)
            out.append((m_new, l_new, acc_new))
        return tuple(out)

    init = (jnp.full((tq, 1), -jnp.inf, F32), jnp.zeros((tq, 1), F32), jnp.zeros((tq, 2 * DIFF_HEAD), F32))
    carry = lax.fori_loop(0, qi, functools.partial(chunk, visible=None), (init, init))
    causal = (lax.broadcasted_iota(jnp.int32, (tq, tq), 1) <= lax.broadcasted_iota(jnp.int32, (tq, tq), 0))
    (_, l0, a0), (_, l1, a1) = chunk(qi, carry, causal)
    o = a0 / l0 - _lambda_value(lam_ref) * (a1 / l1)
    o_ref[0] = _sub_ln(o, subln_ref[...])


def _attn_prompt(q, k, v, lam_vecs, subln):
    b, t, d = q.shape
    hw = 2 * DIFF_HEAD
    tq = _row_tile(t, (344, 256, 128, 64, 8))
    return pl.pallas_call(
        _attn_prompt_kernel,
        grid=(b, d // hw, t // tq),
        in_specs=[
            _full(lam_vecs.shape),
            _full(subln.shape),
            pl.BlockSpec((1, tq, hw), lambda bi, h, qi: (bi, qi, h)),
            pl.BlockSpec((1, t, hw), lambda bi, h, qi: (bi, 0, h)),
            pl.BlockSpec((1, t, hw), lambda bi, h, qi: (bi, 0, h)),
        ],
        out_specs=pl.BlockSpec((1, tq, hw), lambda bi, h, qi: (bi, qi, h)),
        out_shape=jax.ShapeDtypeStruct((b, t, d), F32),
        compiler_params=_cparams("parallel", "parallel", "arbitrary"),
        name="attn_prompt",
    )(lam_vecs, subln, q, k, v)


PAGES_PER_STEP = 4


def _attn_decode_kernel(pt_ref, lam_ref, subln_ref, vspread_ref, vmask_ref, q_ref, *refs):
    del pt_ref
    g = PAGES_PER_STEP
    kc_refs, vc_refs = refs[0:g], refs[g:2 * g]
    kn_ref, vn_ref, o_ref, qblk_ref, qq_ref, m_ref, l_ref, acc_ref = refs[2 * g:]
    p = pl.program_id(1)
    tq = q_ref.shape[1]
    n_maps, _, page = kc_refs[0].shape[1:]
    hw = 2 * DIFF_HEAD
    n_heads = n_maps // 2
    rows = 2 * tq

    @pl.when(p == 0)
    def _():
        m_ref[...] = jnp.full(m_ref.shape, -jnp.inf, F32)
        l_ref[...] = jnp.zeros(l_ref.shape, F32)
        acc_ref[...] = jnp.zeros(acc_ref.shape, F32)
        q = q_ref[0] * (DIFF_HEAD ** -0.5)
        q_rows = jnp.concatenate([q] * n_maps, axis=0)
        row_map = lax.broadcasted_iota(jnp.int32, q_rows.shape, 0) // tq
        lane_map = lax.broadcasted_iota(jnp.int32, q_rows.shape, 1) // DIFF_HEAD
        qblk_ref[...] = jnp.where(row_map == lane_map, q_rows, 0.0).astype(BF16)
        for h in range(n_heads):
            q0, q1 = _split_maps(q[:, h * hw:(h + 1) * hw])
            qq_ref[h] = jnp.concatenate([q0, q1], axis=0)

    def update(s, pv_fn, visible):
        if visible is not None:
            s = [jnp.where(visible, sj, -jnp.inf) for sj in s]
        m_old = m_ref[...]
        m_new = jnp.maximum(m_old, jnp.max(functools.reduce(jnp.maximum, s), axis=-1, keepdims=True))
        pr = [jnp.exp(sj - m_new) for sj in s]
        scale = jnp.exp(m_old - m_new)
        l_ref[...] = scale * l_ref[...] + jnp.sum(functools.reduce(jnp.add, pr), axis=-1, keepdims=True)
        acc_ref[...] = scale * acc_ref[...] + pv_fn([pj.astype(BF16) for pj in pr])
        m_ref[...] = m_new

    def page_scores(kc_ref):
        kt = kc_ref[0].reshape(n_maps * DIFF_HEAD, page).astype(BF16)
        return _dot(qblk_ref[...], kt)

    def page_pv(pr):
        out = None
        for pj, vc_ref in zip(pr, vc_refs):
            vp = vc_ref[0].reshape(page * n_heads, hw).astype(BF16)
            spread = (_dot(pj, vspread_ref[...]) * vmask_ref[...]).astype(BF16)
            d = _dot(spread, vp)
            out = d if out is None else out + d
        return out

    update([page_scores(r) for r in kc_refs], page_pv, None)

    @pl.when(p == pl.num_programs(1) - 1)
    def _():
        q_idx = lax.broadcasted_iota(jnp.int32, (n_heads * rows, page), 0) % tq
        k_idx = lax.broadcasted_iota(jnp.int32, (n_heads * rows, page), 1)
        pad = jnp.zeros((page - tq, hw), BF16)

        def head_block(ref, h):
            return jnp.concatenate([ref[0, :, h * hw:(h + 1) * hw].astype(BF16), pad], axis=0)

        s_new = jnp.concatenate([_dot_nt(qq_ref[h], head_block(kn_ref, h)) for h in range(n_heads)], axis=0)

        def new_pv(pr):
            return jnp.concatenate([_dot(pr[0][h * rows:(h + 1) * rows, :], head_block(vn_ref, h))
                                    for h in range(n_heads)], axis=0)

        update([s_new], new_pv, k_idx <= q_idx)
        lam = _lambda_value(lam_ref)
        on = acc_ref[...] / l_ref[...]
        for h in range(n_heads):
            o = on[h * rows:h * rows + tq, :] - lam * on[h * rows + tq:(h + 1) * rows, :]
            o_ref[0, :, h * hw:(h + 1) * hw] = _sub_ln(o, subln_ref[...])


def _attn_decode(q, cache_k, cache_v, page_table, k_new, v_new, lam_vecs, subln):
    db, tq, d = q.shape
    n_pages = page_table.shape[1]
    _, page, n_maps, _ = cache_k.shape
    hw = 2 * DIFF_HEAD
    n_heads = d // hw
    g = PAGES_PER_STEP
    assert tq % SUBLANES == 0 and tq <= page and page == hw and n_pages % g == 0
    assert cache_k.shape[2:] == (2 * n_heads, DIFF_HEAD) and cache_v.shape[1:] == (page, n_heads, hw)
    cache_k = jnp.transpose(cache_k, (0, 2, 3, 1))

    row_map = jnp.arange(n_maps * tq)[:, None] // tq
    v_col = jnp.arange(page * n_heads)[None, :]
    vspread = (jnp.arange(page)[:, None] == v_col // n_heads).astype(BF16)
    vmask = (row_map // 2 == v_col % n_heads).astype(F32)
    consts = [lam_vecs, subln, vspread, vmask]

    def page_map(j):
        return lambda bi, p, pt: (pt[bi * n_pages + p * g + j], 0, 0, 0)

    def batch_map(bi, p, pt):
        return (bi, 0, 0)

    k_specs = [pl.BlockSpec((1,) + cache_k.shape[1:], page_map(j)) for j in range(g)]
    v_specs = [pl.BlockSpec((1,) + cache_v.shape[1:], page_map(j)) for j in range(g)]
    stat = pltpu.VMEM((n_maps * tq, hw), F32)
    return pl.pallas_call(
        _attn_decode_kernel,
        grid_spec=pltpu.PrefetchScalarGridSpec(
            num_scalar_prefetch=1,
            grid=(db, n_pages // g),
            in_specs=[
                *[pl.BlockSpec(c.shape, lambda bi, p, pt: (0, 0)) for c in consts],
                pl.BlockSpec((1, tq, d), batch_map),
                *k_specs,
                *v_specs,
                pl.BlockSpec((1, tq, d), batch_map),
                pl.BlockSpec((1, tq, d), batch_map),
            ],
            out_specs=pl.BlockSpec((1, tq, d), batch_map),
            scratch_shapes=[pltpu.VMEM((n_maps * tq, d), BF16),
                            pltpu.VMEM((n_heads, 2 * tq, hw), BF16), stat, stat, stat],
        ),
        out_shape=jax.ShapeDtypeStruct((db, tq, d), F32),
        compiler_params=_cparams("parallel", "arbitrary"),
        name="attn_decode",
    )(page_table.reshape(-1), *consts, q, *([cache_k] * g), *([cache_v] * g), k_new, v_new)


def _to_head_batch_lanes(z, b, seq, heads):
    z = z.reshape(b, seq, heads, RWKV_HEAD)
    return jnp.transpose(z, (1, 3, 2, 0)).reshape(seq, RWKV_HEAD, heads * b)


def _from_head_batch_lanes(z, b, seq, heads):
    z = z.reshape(seq, RWKV_HEAD, heads, b)
    return jnp.transpose(z, (3, 0, 2, 1)).reshape(b * seq, heads * RWKV_HEAD)


def _run(x, wkv_in, shift_in, past, pr):
    b, t, d = x.shape
    n = b * t
    heads = d // RWKV_HEAD
    xf = x.reshape(n, d)

    x_prev = jnp.concatenate([shift_in[:, None, :], x[:, :-1]], axis=1).reshape(n, d)
    time_major = b != LANES
    r, w, k, v, a, bb, g = _rwkv_pre(xf, x_prev, pr["rwkv"], b, t, time_major)
    if not time_major:
        scan_ops = [_to_head_batch_lanes(z, b, t, heads) for z in (r, w, k, v, a, bb)]
        o_l, s_fin = _rwkv_scan_native_state(*scan_ops, wkv_in.reshape(b, -1))
        o = _from_head_batch_lanes(o_l, b, t, heads)
        new_wkv = s_fin.reshape(wkv_in.shape)[:, None]
    else:
        s0 = jnp.transpose(wkv_in.reshape(b, heads // 2, 2, RWKV_HEAD, RWKV_HEAD), (4, 3, 2, 1, 0))
        o3, s_fin = _rwkv_scan(*[z.reshape(t, b, d) for z in (r, w, k, v, a, bb)],
                               s0.reshape(RWKV_HEAD, RWKV_HEAD, heads * b))
        o = o3.reshape(t, b * d)
        s_fin = s_fin.reshape(RWKV_HEAD, RWKV_HEAD, 2, heads // 2, b)
        new_wkv = jnp.transpose(s_fin, (4, 3, 2, 1, 0)).reshape(wkv_in.shape)[:, None]
    new_shift = x[:, -1][:, None]
    x1 = _rwkv_post(o, r, k, v, g, xf, pr["rwkv"], pr["post_ln"][0][0], t, time_major)
    x2 = _moe_ln(x1, pr["post_ln"][0][1], pr["router_wt"], pr["router_b"], *pr["moe"][0])

    k_new, v_new, q = _proj(x2, [pr["w_k_shared"], pr["w_v_shared"], pr["diff_w_q"]], "kv_q_proj")
    k3, v3, q3 = (z.reshape(b, t, d) for z in (k_new, v_new, q))
    if past is None:
        att = _attn_prompt(q3, k3, v3, pr["diff_lambda"], pr["diff_subln"])
    else:
        cache_k, cache_v, page_table = past
        att = _attn_decode(q3, cache_k, cache_v, page_table, k3, v3, pr["diff_lambda"], pr["diff_subln"])
    x3 = _proj_ln(att.reshape(n, d), x2, pr["diff_w_out"], pr["post_ln"][1][0])
    x4 = _moe_ln(x3, pr["post_ln"][1][1], pr["router_wt"], pr["router_b"], *pr["moe"][1])

    n_diff = d // (2 * DIFF_HEAD)
    return (x4.reshape(b, t, d), k_new.reshape(b, t, 2 * n_diff, DIFF_HEAD),
            v_new.reshape(b, t, n_diff, 2 * DIFF_HEAD), new_wkv, new_shift)


def kernel(x_prompt, x_sample, cache_k, cache_v, page_table, state_wkv, state_shift, meta_tokens,
           rwkv_mix, rwkv_w_rkv, rwkv_decay_w0, rwkv_decay_w1, rwkv_decay_w2, rwkv_a_w0, rwkv_a_w1,
           rwkv_a_w2, rwkv_g_w1, rwkv_g_w2, rwkv_kk_ka, rwkv_r_k, rwkv_lnx, rwkv_w_out, w_kv_shared,
           diff_w_q, diff_lambda, diff_subln, diff_w_out, router_w, router_b, moe_w_gate, moe_w_up,
           moe_w_down, post_ln):
    d = x_prompt.shape[-1]
    assert rwkv_mix.shape[0] == N_A_LAYERS and moe_w_gate.shape[0] == DEPTH
    ind, ind_t = _head_indicator(d, RWKV_HEAD)
    qk_width = diff_w_q.shape[-1]
    pr = {
        "rwkv": {
            "mix": rwkv_mix[0], "w_r": rwkv_w_rkv[0, 0].astype(BF16), "w_k": rwkv_w_rkv[0, 1].astype(BF16),
            "w_v": rwkv_w_rkv[0, 2].astype(BF16), "dw0": rwkv_decay_w0, "dw1": rwkv_decay_w1[0].astype(BF16),
            "dw2": rwkv_decay_w2[0].astype(BF16), "aw0": rwkv_a_w0, "aw1": rwkv_a_w1[0].astype(BF16),
            "aw2": rwkv_a_w2[0].astype(BF16), "gw1": rwkv_g_w1[0].astype(BF16),
            "gw2": rwkv_g_w2[0].astype(BF16), "kk_ka": rwkv_kk_ka[0], "r_k": rwkv_r_k[0].reshape(1, d),
            "lnx": rwkv_lnx[0], "w_out": rwkv_w_out[0].astype(BF16), "ind": ind, "ind_t": ind_t,
        },
        "post_ln": post_ln,
        "router_wt": router_w.T,
        "router_b": router_b.reshape(N_EXPERTS, 1),
        "moe": [(moe_w_gate[l].astype(BF16), moe_w_up[l].astype(BF16), moe_w_down[l].astype(BF16))
                for l in range(DEPTH)],
        "w_k_shared": w_kv_shared[:, :qk_width].astype(BF16),
        "w_v_shared": w_kv_shared[:, qk_width:].astype(BF16),
        "diff_w_q": diff_w_q[0].astype(BF16),
        "diff_lambda": diff_lambda[0],
        "diff_subln": diff_subln[0].reshape(1, -1),
        "diff_w_out": diff_w_out[0].astype(BF16),
    }

    bp = x_prompt.shape[0]
    n_meta = meta_tokens.shape[0]
    meta = jnp.broadcast_to(meta_tokens[None].astype(x_prompt.dtype), (bp, n_meta, d))
    xp = jnp.concatenate([meta, x_prompt], axis=1)
    wkv0 = jnp.zeros((bp,) + state_wkv.shape[2:], x_prompt.dtype)
    shift0 = jnp.zeros((bp, d), x_prompt.dtype)
    yp, k_p, v_p, wkv_p, shift_p = _run(xp, wkv0, shift0, None, pr)

    past = (cache_k, cache_v, page_table)
    ys, k_s, v_s, wkv_s, shift_s = _run(x_sample, state_wkv[:, 0], state_shift[:, 0], past, pr)
    return (yp[:, n_meta:], ys, k_p, v_p, wkv_p, shift_p, k_s, v_s, wkv_s, shift_s)
```

```python
import functools
import math

import jax
import jax.numpy as jnp
from jax import lax
from jax.experimental import pallas as pl
from jax.experimental.pallas import tpu as pltpu

F32 = jnp.float32
BF16 = jnp.bfloat16

DEPTH = 2
N_A_LAYERS = 1
RWKV_HEAD = 64
DIFF_HEAD = 64
LNX_EPS = 64e-5
SUBLN_EPS = 1e-5
LN_EPS = 1e-5
N_EXPERTS = 16
N_GROUPS = 4
EXPERTS_PER_GROUP = N_EXPERTS // N_GROUPS
TOP_K = 2
DEEPNORM_ALPHA = (2 * DEPTH) ** 0.25
LAMBDA_INIT = 0.8 - 0.6 * math.exp(-0.3 * 1)

LANES = 128
SUBLANES = 8
VMEM_LIMIT_BYTES = 56 * 1024 * 1024


def _cparams(*sem):
    return pltpu.CompilerParams(dimension_semantics=sem, vmem_limit_bytes=VMEM_LIMIT_BYTES)


def _row_tile(n, candidates):
    for c in candidates:
        if n % c == 0:
            return c
    raise ValueError(f"no row tile for {n} in {candidates}")


def _full(shape):
    zeros = (0,) * len(shape)
    return pl.BlockSpec(shape, lambda *_: zeros)


def _dot(a, b):
    return jnp.dot(a, b, preferred_element_type=F32)


def _dot_nt(a, b):
    return lax.dot_general(a, b, (((1,), (1,)), ((), ())), preferred_element_type=F32)


def _split_dot(x, w_bf16):
    hi = x.astype(BF16)
    lo = (x - hi.astype(F32)).astype(BF16)
    return _dot(hi, w_bf16) + _dot(lo, w_bf16)


def _layernorm(z, g, b):
    mu = jnp.mean(z, axis=-1, keepdims=True)
    zc = z - mu
    var = jnp.mean(zc * zc, axis=-1, keepdims=True)
    return zc * lax.rsqrt(var + LN_EPS) * g + b


def _head_indicator(d, head):
    n_heads = d // head
    assert n_heads <= LANES
    ind = (jnp.arange(d)[:, None] // head == jnp.arange(LANES)[None, :]).astype(BF16)
    return ind, ind.T


def _rwkv_pre_kernel(x_ref, xp_ref, mix_ref, wr_ref, wk_ref, wv_ref, dw0_ref, dw1_ref, dw2_ref,
                     aw0_ref, aw1_ref, aw2_ref, gw1_ref, gw2_ref, kkka_ref, ind_ref, indt_ref,
                     r_ref, w_ref, k_ref, v_ref, a_ref, b_ref, g_ref):
    x = x_ref[...]
    xx = xp_ref[...] - x

    def mixed(i):
        return (x + xx * mix_ref[i:i + 1, :]).astype(BF16)

    r = _dot(mixed(0), wr_ref[...])
    k = _dot(mixed(2), wk_ref[...])
    v = _dot(mixed(3), wv_ref[...])
    lw = jnp.tanh(_dot(mixed(1), dw1_ref[...])).astype(BF16)
    z = -(dw0_ref[...] + _dot(lw, dw2_ref[...]))
    softplus = jnp.maximum(z, 0.0) + jnp.log(1.0 + jnp.exp(-jnp.abs(z)))
    w_log = -softplus - 0.5
    decay = jnp.exp(-jnp.exp(w_log))
    la = _dot(mixed(4), aw1_ref[...]).astype(BF16)
    a = jax.nn.sigmoid(aw0_ref[...] + _dot(la, aw2_ref[...]))
    lg = jax.nn.sigmoid(_dot(mixed(5), gw1_ref[...])).astype(BF16)
    g = _dot(lg, gw2_ref[...])

    kk = k * kkka_ref[0:1, :]
    ss = _split_dot(_split_dot(kk * kk, ind_ref[...]), indt_ref[...])
    kk = kk / jnp.maximum(jnp.sqrt(ss), 1e-12)
    k = k * (1.0 + (a - 1.0) * kkka_ref[1:2, :])

    r_ref[...] = r
    w_ref[...] = decay
    k_ref[...] = k
    v_ref[...] = v
    a_ref[...] = -kk
    b_ref[...] = kk * a
    g_ref[...] = g


def _time_major_spec(tm, d, tiles_per_seq):
    return pl.BlockSpec((tm, d), lambda i: (i % tiles_per_seq, i // tiles_per_seq))


def _rwkv_pre(x, x_prev, p, b, t, time_major):
    n, d = x.shape
    tm = _row_tile(t, (344, 256, 128, 64, 8)) if time_major else _row_tile(n, (192, 128, 64, 8))
    row = pl.BlockSpec((tm, d), lambda i: (i, 0))
    seq_spec = _time_major_spec(tm, d, t // tm) if time_major else row
    seq_shape = (t, b * d) if time_major else (n, d)
    weights = [p["mix"], p["w_r"], p["w_k"], p["w_v"], p["dw0"], p["dw1"], p["dw2"], p["aw0"],
               p["aw1"], p["aw2"], p["gw1"], p["gw2"], p["kk_ka"], p["ind"], p["ind_t"]]
    return pl.pallas_call(
        _rwkv_pre_kernel,
        grid=(n // tm,),
        in_specs=[row, row] + [_full(w.shape) for w in weights],
        out_specs=[seq_spec] * 6 + [row],
        out_shape=[jax.ShapeDtypeStruct(seq_shape, F32)] * 6 + [jax.ShapeDtypeStruct((n, d), F32)],
        compiler_params=_cparams("parallel"),
        name="rwkv_pre",
    )(x, x_prev, *weights)


SCAN_VALUE_SPLIT = 2


def _scan_groups(s_ref):
    nv = s_ref.shape[1] // SCAN_VALUE_SPLIT
    return nv, range(0, s_ref.shape[1], nv)


def _state_times(s_ref, vec_ref):
    nv, groups = _scan_groups(s_ref)
    out = []
    for lo in groups:
        acc = [jnp.zeros((nv, LANES), F32), jnp.zeros((nv, LANES), F32)]
        for j in range(s_ref.shape[0]):
            acc[j % 2] = acc[j % 2] + s_ref[j, lo:lo + nv, :] * vec_ref[pl.ds(j, 1), :]
        out.append(acc[0] + acc[1])
    return tuple(out)


def _scan_step(s_ref, r_ref, w_ref, k_ref, v_ref, b_ref, a_next_ref, o_ref, sa_all):
    nv, groups = _scan_groups(s_ref)
    sa_next = []
    for g, lo in enumerate(groups):
        vt = v_ref[lo:lo + nv, :]
        sa = sa_all[g]
        ot = [jnp.zeros_like(vt), jnp.zeros_like(vt)]
        sn = [jnp.zeros_like(vt), jnp.zeros_like(vt)]
        for j in range(s_ref.shape[0]):
            s = (s_ref[j, lo:lo + nv, :] * w_ref[pl.ds(j, 1), :] + sa * b_ref[pl.ds(j, 1), :]
                 + vt * k_ref[pl.ds(j, 1), :])
            s_ref[j, lo:lo + nv, :] = s
            ot[j % 2] = ot[j % 2] + s * r_ref[pl.ds(j, 1), :]
            sn[j % 2] = sn[j % 2] + s * a_next_ref[pl.ds(j, 1), :]
        o_ref[lo:lo + nv, :] = ot[0] + ot[1]
        sa_next.append(sn[0] + sn[1])
    return tuple(sa_next)


def _rwkv_scan_kernel(r_ref, w_ref, k_ref, v_ref, a_ref, b_ref, s0_ref, o_ref, sout_ref, s_ref,
                      buf_a, buf_b, obuf_a, obuf_b):
    tc = pl.program_id(0)
    steps, nb, d = r_ref.shape
    n_chunks = d // LANES
    half = LANES // 2
    srcs = (r_ref, w_ref, k_ref, v_ref, a_ref, b_ref)

    @pl.when(tc == 0)
    def _():
        s_ref[...] = s0_ref[...]

    def to_lanes(t, buf):
        for i_op, src in enumerate(srcs):
            x = src[t]
            rows = jnp.concatenate([x[:, i * LANES:(i + 1) * LANES] for i in range(n_chunks)], axis=0)
            cols = rows.T
            buf[i_op] = jnp.concatenate([cols[:half], cols[half:]], axis=1)

    def from_lanes(obuf, t):
        y = obuf[...]
        rows = jnp.concatenate([y[:, :half], y[:, half:]], axis=0).T
        for i in range(n_chunks):
            o_ref[t, :, i * LANES:(i + 1) * LANES] = rows[i * nb:(i + 1) * nb, :]

    def step(cur, nxt, obuf, sa):
        r_b, w_b, k_b, v_b, _, b_b = (cur.at[i] for i in range(6))
        return _scan_step(s_ref, r_b, w_b, k_b, v_b, b_b, nxt.at[4], obuf, sa)

    def pair(i, sa):
        t = 2 * i
        to_lanes(t + 1, buf_b)
        from_lanes(obuf_b, jnp.maximum(t - 1, 0))
        sa = step(buf_a, buf_b, obuf_a, sa)
        to_lanes(jnp.minimum(t + 2, steps - 1), buf_a)
        from_lanes(obuf_a, t)
        return step(buf_b, buf_a, obuf_b, sa)

    to_lanes(0, buf_a)
    obuf_b[...] = jnp.zeros(obuf_b.shape, F32)
    lax.fori_loop(0, steps // 2, pair, _state_times(s_ref, buf_a.at[4]))
    from_lanes(obuf_b, steps - 1)

    @pl.when(tc == pl.num_programs(0) - 1)
    def _():
        sout_ref[...] = s_ref[...]


def _rwkv_scan_native_state_kernel(r_ref, w_ref, k_ref, v_ref, a_ref, b_ref, s0_ref, o_ref, sout_ref,
                                   s_ref, tmp_ref):
    steps = r_ref.shape[0]
    nk = s_ref.shape[0]
    tmp_ref[...] = s0_ref[...].T
    for j in range(nk):
        s_ref[j] = tmp_ref[pl.ds(j, nk, stride=nk), :]

    def step(t, sa):
        return _scan_step(s_ref, r_ref.at[t], w_ref.at[t], k_ref.at[t], v_ref.at[t], b_ref.at[t],
                          a_ref.at[jnp.minimum(t + 1, steps - 1)], o_ref.at[t], sa)

    lax.fori_loop(0, steps, step, _state_times(s_ref, a_ref.at[0]))
    for j in range(nk):
        tmp_ref[pl.ds(j, nk, stride=nk), :] = s_ref[j]
    sout_ref[...] = tmp_ref[...].T


def _rwkv_scan_native_state(r, w, k, v, a, b, state):
    t, nk, l = r.shape
    nb, width = state.shape
    assert nb == LANES and width == (l // LANES) * nk * nk
    seq = pl.BlockSpec((t, nk, LANES), lambda h: (0, 0, h))
    st = pl.BlockSpec((LANES, nk * nk), lambda h: (0, h))
    return pl.pallas_call(
        _rwkv_scan_native_state_kernel,
        grid=(l // LANES,),
        in_specs=[seq] * 6 + [st],
        out_specs=[seq, st],
        out_shape=[jax.ShapeDtypeStruct((t, nk, l), F32), jax.ShapeDtypeStruct(state.shape, F32)],
        scratch_shapes=[pltpu.VMEM((nk, nk, LANES), F32), pltpu.VMEM((nk * nk, LANES), F32)],
        compiler_params=_cparams("parallel"),
        name="rwkv_scan_native_state",
    )(r, w, k, v, a, b, state)


def _rwkv_scan(r, w, k, v, a, b, s0):
    t, nb, d = r.shape
    nk = RWKV_HEAD
    assert nb * (d // LANES) == LANES // 2 and d // nk == 2 * (d // LANES)
    steps = _row_tile(t, (48, 16, 8, 4, 2))
    seq = pl.BlockSpec((steps, nb, d), lambda c: (c, 0, 0))
    st = pl.BlockSpec((nk, nk, LANES), lambda c: (0, 0, 0))
    operands = pltpu.VMEM((6, nk, LANES), F32)
    readout = pltpu.VMEM((nk, LANES), F32)
    return pl.pallas_call(
        _rwkv_scan_kernel,
        grid=(t // steps,),
        in_specs=[seq] * 6 + [st],
        out_specs=[seq, st],
        out_shape=[jax.ShapeDtypeStruct((t, nb, d), F32), jax.ShapeDtypeStruct((nk, nk, LANES), F32)],
        scratch_shapes=[pltpu.VMEM((nk, nk, LANES), F32), operands, operands, readout, readout],
        compiler_params=_cparams("arbitrary"),
        name="rwkv_scan",
    )(r, w, k, v, a, b, s0)


def _rwkv_post_kernel(o_ref, r_ref, k_ref, v_ref, g_ref, x_ref, lnx_ref, rk_ref, ind_ref, indt_ref,
                      wout_ref, ln_ref, out_ref):
    ind = ind_ref[...]
    ind_t = indt_ref[...]
    inv_head = 1.0 / RWKV_HEAD
    o = o_ref[...]
    mu = _split_dot(_split_dot(o, ind) * inv_head, ind_t)
    oc = o - mu
    var = _split_dot(oc * oc, ind) * inv_head
    rstd = _split_dot(lax.rsqrt(var + LNX_EPS), ind_t)
    on = oc * rstd * lnx_ref[0:1, :] + lnx_ref[1:2, :]
    v = v_ref[...]
    bonus = _split_dot(_split_dot(r_ref[...] * k_ref[...] * rk_ref[...], ind), ind_t)
    on = on + bonus * v
    h = _dot((on * g_ref[...]).astype(BF16), wout_ref[...])
    out_ref[...] = _layernorm(DEEPNORM_ALPHA * x_ref[...] + h, ln_ref[0:1, :], ln_ref[1:2, :])


def _rwkv_post(o, r, k, v, g, x, p, ln, t, time_major):
    n, d = x.shape
    tm = _row_tile(t, (344, 256, 128, 64, 8)) if time_major else _row_tile(n, (384, 256, 128, 64, 8))
    row = pl.BlockSpec((tm, d), lambda i: (i, 0))
    seq_spec = _time_major_spec(tm, d, t // tm) if time_major else row
    weights = [p["lnx"], p["r_k"], p["ind"], p["ind_t"], p["w_out"], ln]
    return pl.pallas_call(
        _rwkv_post_kernel,
        grid=(n // tm,),
        in_specs=[seq_spec] * 4 + [row] * 2 + [_full(w.shape) for w in weights],
        out_specs=row,
        out_shape=jax.ShapeDtypeStruct((n, d), F32),
        compiler_params=_cparams("parallel"),
        name="rwkv_post",
    )(o, r, k, v, g, x, *weights)


def _proj_kernel(*refs, n_out):
    x_ref = refs[0]
    w_refs = refs[1:1 + n_out]
    out_refs = refs[1 + n_out:]
    x = x_ref[...].astype(BF16)
    for w_ref, out_ref in zip(w_refs, out_refs):
        out_ref[...] = _dot(x, w_ref[...])


def _proj(x, ws, name):
    n, d = x.shape
    tm = _row_tile(n, (384, 256, 128, 64, 8))
    return pl.pallas_call(
        functools.partial(_proj_kernel, n_out=len(ws)),
        grid=(n // tm,),
        in_specs=[pl.BlockSpec((tm, d), lambda i: (i, 0))] + [_full(w.shape) for w in ws],
        out_specs=[pl.BlockSpec((tm, w.shape[1]), lambda i: (i, 0)) for w in ws],
        out_shape=[jax.ShapeDtypeStruct((n, w.shape[1]), F32) for w in ws],
        compiler_params=_cparams("parallel"),
        name=name,
    )(x, *ws)


def _proj_ln_kernel(a_ref, x_ref, w_ref, ln_ref, out_ref):
    h = _dot(a_ref[...].astype(BF16), w_ref[...])
    out_ref[...] = _layernorm(DEEPNORM_ALPHA * x_ref[...] + h, ln_ref[0:1, :], ln_ref[1:2, :])


def _proj_ln(a, x, w, ln):
    n, d = x.shape
    tm = _row_tile(n, (384, 256, 128, 64, 8))
    row = pl.BlockSpec((tm, d), lambda i: (i, 0))
    return pl.pallas_call(
        _proj_ln_kernel,
        grid=(n // tm,),
        in_specs=[pl.BlockSpec((tm, a.shape[1]), lambda i: (i, 0)), row, _full(w.shape), _full(ln.shape)],
        out_specs=row,
        out_shape=jax.ShapeDtypeStruct((n, d), F32),
        compiler_params=_cparams("parallel"),
        name="proj_ln",
    )(a, x, w, ln)


MOE_TILE = 512


def _router_kernel(x_ref, wt_ref, bias_ref, tri_ref, eidx_ref, wts_ref, rank_ref, cnt_ref, run_ref):
    step = pl.program_id(0)

    @pl.when(step == 0)
    def _():
        run_ref[...] = jnp.zeros(run_ref.shape, F32)

    x = x_ref[...]
    x_hi = x.astype(BF16)
    x_lo = (x - x_hi.astype(F32)).astype(BF16)
    wt = wt_ref[...]
    wt_hi = wt.astype(BF16)
    wt_lo = (wt - wt_hi.astype(F32)).astype(BF16)
    logits = _dot_nt(wt_hi, x_hi) + _dot_nt(wt_hi, x_lo) + _dot_nt(wt_lo, x_hi)
    s = jax.nn.sigmoid(logits)
    sel = s + bias_ref[...]
    s_rows = [s[e:e + 1, :] for e in range(N_EXPERTS)]
    sel_rows = [sel[e:e + 1, :] for e in range(N_EXPERTS)]

    best_grp = None
    best_idx = None
    for gi in range(N_GROUPS):
        rows = sel_rows[gi * EXPERTS_PER_GROUP:(gi + 1) * EXPERTS_PER_GROUP]
        score = None
        for i in range(EXPERTS_PER_GROUP):
            for j in range(i + 1, EXPERTS_PER_GROUP):
                pair = rows[i] + rows[j]
                score = pair if score is None else jnp.maximum(score, pair)
        if best_grp is None:
            best_grp = score
            best_idx = jnp.zeros(score.shape, jnp.int32)
        else:
            better = score > best_grp
            best_grp = jnp.where(better, score, best_grp)
            best_idx = jnp.where(better, gi, best_idx)

    neg_inf = jnp.float32(-jnp.inf)
    masked = [jnp.where(best_idx == e // EXPERTS_PER_GROUP, sel_rows[e], neg_inf)
              for e in range(N_EXPERTS)]

    def argmax_rows(exclude):
        top = jnp.full(masked[0].shape, neg_inf)
        idx = jnp.full(masked[0].shape, -1, jnp.int32)
        val = jnp.zeros(masked[0].shape, F32)
        for e in range(N_EXPERTS):
            cand = masked[e] if exclude is None else jnp.where(exclude == e, neg_inf, masked[e])
            better = cand > top
            top = jnp.where(better, cand, top)
            idx = jnp.where(better, e, idx)
            val = jnp.where(better, s_rows[e], val)
        return idx, val

    i1, w1 = argmax_rows(None)
    i2, w2 = argmax_rows(i1)
    tot = w1 + w2
    eidx_ref[0:1, :] = i1
    eidx_ref[1:2, :] = i2
    wts_ref[0:1, :] = w1 / tot
    wts_ref[1:2, :] = w2 / tot

    e_iota = lax.broadcasted_iota(jnp.int32, logits.shape, 0)
    run = run_ref[...]
    for slot, idx in enumerate((i1, i2)):
        onehot = jnp.where(e_iota == idx, 1.0, 0.0)
        prefix = _dot(onehot.astype(BF16), tri_ref[...])
        rank = jnp.sum(onehot * (run[:, 0:1] + prefix - 1.0), axis=0, keepdims=True)
        rank_ref[slot:slot + 1, :] = rank.astype(jnp.int32)
        run = run + jnp.sum(onehot, axis=-1, keepdims=True)
    run_ref[...] = run
    cnt_ref[...] = run


def _router(x, router_wt, router_b):
    n, d = x.shape
    tm = _row_tile(n, (384, 256, 128))
    tri = (jnp.arange(tm)[:, None] <= jnp.arange(tm)[None, :]).astype(BF16)
    pair = pl.BlockSpec((TOP_K, tm), lambda i: (0, i))
    return pl.pallas_call(
        _router_kernel,
        grid=(n // tm,),
        in_specs=[pl.BlockSpec((tm, d), lambda i: (i, 0)), _full(router_wt.shape), _full(router_b.shape),
                  _full(tri.shape)],
        out_specs=[pair, pair, pair, _full((N_EXPERTS, LANES))],
        out_shape=[jax.ShapeDtypeStruct((TOP_K, n), jnp.int32), jax.ShapeDtypeStruct((TOP_K, n), F32),
                   jax.ShapeDtypeStruct((TOP_K, n), jnp.int32), jax.ShapeDtypeStruct((N_EXPERTS, LANES), F32)],
        scratch_shapes=[pltpu.VMEM((N_EXPERTS, LANES), F32)],
        compiler_params=_cparams("arbitrary"),
        name="router",
    )(x, router_wt, router_b, tri)


DMA_UNROLL = 8


def _dispatch_kernel(pos_ref, x_ref, init_hbm, xs_hbm, sem):
    del init_hbm
    tm = x_ref.shape[0]

    def row_copy(slot, r):
        return pltpu.make_async_copy(x_ref.at[pl.ds(r, 1)], xs_hbm.at[pl.ds(pos_ref[0, slot, r], 1)], sem)

    def start(r, c):
        for slot in range(TOP_K):
            row_copy(slot, r).start()
        return c

    def wait(r, c):
        for slot in range(TOP_K):
            row_copy(slot, r).wait()
        return c

    lax.fori_loop(0, tm, start, 0, unroll=DMA_UNROLL)
    lax.fori_loop(0, tm, wait, 0, unroll=DMA_UNROLL)


def _dispatch(x, pos_blocks, n_rows):
    n, d = x.shape
    nt, _, tm = pos_blocks.shape
    return pl.pallas_call(
        _dispatch_kernel,
        grid=(nt,),
        in_specs=[pl.BlockSpec((1, TOP_K, tm), lambda i: (i, 0, 0), memory_space=pltpu.SMEM),
                  pl.BlockSpec((tm, d), lambda i: (i, 0)),
                  pl.BlockSpec(memory_space=pl.ANY)],
        out_specs=pl.BlockSpec(memory_space=pl.ANY),
        out_shape=jax.ShapeDtypeStruct((n_rows, d), F32),
        scratch_shapes=[pltpu.SemaphoreType.DMA(())],
        input_output_aliases={2: 0},
        compiler_params=_cparams("arbitrary"),
        name="moe_dispatch",
    )(pos_blocks, x, jnp.zeros((n_rows, d), F32))


def _experts_kernel(tile_e_ref, n_used_ref, xs_ref, wg_ref, wu_ref, wd_ref, ys_ref):
    del tile_e_ref
    i = pl.program_id(0)

    @pl.when(i < n_used_ref[0])
    def _():
        x = xs_ref[...].astype(BF16)
        hg = _dot(x, wg_ref[0])
        hu = _dot(x, wu_ref[0])
        h = (hg * jax.nn.sigmoid(hg) * hu).astype(BF16)
        ys_ref[...] = _dot(h, wd_ref[0])

    @pl.when(i >= n_used_ref[0])
    def _():
        ys_ref[...] = jnp.zeros(ys_ref.shape, F32)


def _experts(xs, tile_e, n_used, wg, wu, wd):
    n_rows, d = xs.shape
    d_exp = wg.shape[2]
    tm = MOE_TILE

    def w_map(i, te, nu):
        return (te[i], 0, 0)

    return pl.pallas_call(
        _experts_kernel,
        grid_spec=pltpu.PrefetchScalarGridSpec(
            num_scalar_prefetch=2,
            grid=(n_rows // tm,),
            in_specs=[
                pl.BlockSpec((tm, d), lambda i, te, nu: (jnp.minimum(i, nu[0] - 1), 0)),
                pl.BlockSpec((1, d, d_exp), w_map),
                pl.BlockSpec((1, d, d_exp), w_map),
                pl.BlockSpec((1, d_exp, d), w_map),
            ],
            out_specs=pl.BlockSpec((tm, d), lambda i, te, nu: (i, 0)),
        ),
        out_shape=jax.ShapeDtypeStruct((n_rows, d), F32),
        compiler_params=_cparams("arbitrary"),
        name="moe_experts",
    )(tile_e, n_used, xs, wg, wu, wd)


def _combine_ln_kernel(pos_ref, wts_ref, x_ref, ln_ref, ys_hbm, out_ref, ybuf, sem):
    tm = x_ref.shape[0]

    def row_copy(slot, r):
        return pltpu.make_async_copy(ys_hbm.at[pl.ds(pos_ref[0, slot, r], 1)], ybuf.at[slot, pl.ds(r, 1)], sem)

    def start(r, c):
        for slot in range(TOP_K):
            row_copy(slot, r).start()
        return c

    def wait(r, c):
        for slot in range(TOP_K):
            row_copy(slot, r).wait()
        return c

    lax.fori_loop(0, tm, start, 0, unroll=DMA_UNROLL)
    lax.fori_loop(0, tm, wait, 0, unroll=DMA_UNROLL)
    y = wts_ref[:, 0:1] * ybuf[0] + wts_ref[:, 1:2] * ybuf[1]
    out_ref[...] = _layernorm(DEEPNORM_ALPHA * x_ref[...] + y, ln_ref[0:1, :], ln_ref[1:2, :])


def _combine_ln(ys, pos_blocks, wts_t, x, ln):
    n, d = x.shape
    nt, _, tm = pos_blocks.shape
    row = pl.BlockSpec((tm, d), lambda i: (i, 0))
    return pl.pallas_call(
        _combine_ln_kernel,
        grid=(nt,),
        in_specs=[pl.BlockSpec((1, TOP_K, tm), lambda i: (i, 0, 0), memory_space=pltpu.SMEM),
                  pl.BlockSpec((tm, TOP_K), lambda i: (i, 0)), row, _full(ln.shape),
                  pl.BlockSpec(memory_space=pl.ANY)],
        out_specs=row,
        out_shape=jax.ShapeDtypeStruct((n, d), F32),
        scratch_shapes=[pltpu.VMEM((TOP_K, tm, d), F32), pltpu.SemaphoreType.DMA(())],
        compiler_params=_cparams("arbitrary"),
        name="moe_combine_ln",
    )(pos_blocks, wts_t, x, ln, ys)


def _moe_ln(x, ln, router_wt, router_b, wg, wu, wd):
    n, d = x.shape
    eidx, wts, rank, cnt = _router(x, router_wt, router_b)

    n_tiles = pl.cdiv(TOP_K * n, MOE_TILE) + N_EXPERTS
    counts = cnt[:, 0].astype(jnp.int32)
    tiles_per_e = (counts + MOE_TILE - 1) // MOE_TILE
    tile_end = jnp.cumsum(tiles_per_e)
    tile_start = tile_end - tiles_per_e
    n_used = tile_end[-1:]
    experts = jnp.arange(N_EXPERTS, dtype=jnp.int32)
    base = jnp.sum(jnp.where(eidx[..., None] == experts, tile_start * MOE_TILE, 0), axis=-1)
    pos = base + rank
    tile_ids = jnp.minimum(jnp.arange(n_tiles, dtype=jnp.int32), n_used - 1)
    tile_e = jnp.sum((tile_end[None, :] <= tile_ids[:, None]).astype(jnp.int32), axis=1)
    tile_e = jnp.minimum(tile_e, N_EXPERTS - 1)

    tm = _row_tile(n, (384, 256, 128))
    pos_blocks = jnp.transpose(pos.reshape(TOP_K, n // tm, tm), (1, 0, 2))
    xs = _dispatch(x, pos_blocks, n_tiles * MOE_TILE)
    ys = _experts(xs, tile_e, n_used, wg, wu, wd)
    return _combine_ln(ys, pos_blocks, wts.T, x, ln)


def _lambda_value(lam_ref):
    lf = lam_ref[...]
    s01 = jnp.sum(lf[0:1, :] * lf[1:2, :], axis=-1, keepdims=True)
    s23 = jnp.sum(lf[2:3, :] * lf[3:4, :], axis=-1, keepdims=True)
    return jnp.exp(s01) - jnp.exp(s23) + LAMBDA_INIT


def _sub_ln(o, subln):
    o = o * lax.rsqrt(jnp.mean(o * o, axis=-1, keepdims=True) + SUBLN_EPS) * subln
    return o * (1.0 - LAMBDA_INIT)


def _split_maps(q):
    lane = lax.broadcasted_iota(jnp.int32, q.shape, 1)
    q0 = jnp.where(lane < DIFF_HEAD, q, 0.0).astype(BF16)
    q1 = jnp.where(lane >= DIFF_HEAD, q, 0.0).astype(BF16)
    return q0, q1


def _attn_prompt_kernel(lam_ref, subln_ref, q_ref, k_ref, v_ref, o_ref):
    qi = pl.program_id(2)
    tq = q_ref.shape[1]
    q0, q1 = _split_maps(q_ref[0] * (DIFF_HEAD ** -0.5))

    def chunk(j, carry, visible):
        start = pl.multiple_of(j * tq, SUBLANES)
        kc = k_ref[0, pl.ds(start, tq), :].astype(BF16)
        vc = v_ref[0, pl.ds(start, tq), :].astype(BF16)
        out = []
        for qm, (m, l, acc) in zip((q0, q1), carry):
            s = _dot_nt(qm, kc)
            if visible is not None:
                s = jnp.where(visible, s, -jnp.inf)
            m_new = jnp.maximum(m, jnp.max(s, axis=-1, keepdims=True))
            p = jnp.exp(s - m_new)
            scale = jnp.exp(m - m_new)
            l_new = scale * l + jnp.sum(p, axis=-1, keepdims=True)
            acc_new = scale * acc + _dot(p.astype(BF16), vc)
            out.append((m_new, l_new, acc_new))
        return tuple(out)

    init = (jnp.full((tq, 1), -jnp.inf, F32), jnp.zeros((tq, 1), F32), jnp.zeros((tq, 2 * DIFF_HEAD), F32))
    carry = lax.fori_loop(0, qi, functools.partial(chunk, visible=None), (init, init))
    causal = (lax.broadcasted_iota(jnp.int32, (tq, tq), 1) <= lax.broadcasted_iota(jnp.int32, (tq, tq), 0))
    (_, l0, a0), (_, l1, a1) = chunk(qi, carry, causal)
    o = a0 / l0 - _lambda_value(lam_ref) * (a1 / l1)
    o_ref[0] = _sub_ln(o, subln_ref[...])


def _attn_prompt(q, k, v, lam_vecs, subln):
    b, t, d = q.shape
    hw = 2 * DIFF_HEAD
    tq = _row_tile(t, (344, 256, 128, 64, 8))
    return pl.pallas_call(
        _attn_prompt_kernel,
        grid=(b, d // hw, t // tq),
        in_specs=[
            _full(lam_vecs.shape),
            _full(subln.shape),
            pl.BlockSpec((1, tq, hw), lambda bi, h, qi: (bi, qi, h)),
            pl.BlockSpec((1, t, hw), lambda bi, h, qi: (bi, 0, h)),
            pl.BlockSpec((1, t, hw), lambda bi, h, qi: (bi, 0, h)),
        ],
        out_specs=pl.BlockSpec((1, tq, hw), lambda bi, h, qi: (bi, qi, h)),
        out_shape=jax.ShapeDtypeStruct((b, t, d), F32),
        compiler_params=_cparams("parallel", "parallel", "arbitrary"),
        name="attn_prompt",
    )(lam_vecs, subln, q, k, v)


PAGES_PER_STEP = 8


def _attn_decode_kernel(pt_ref, lam_ref, subln_ref, vspread_ref, vmask_ref, q_ref, *refs):
    del pt_ref
    g = PAGES_PER_STEP
    kc_refs, vc_refs = refs[0:g], refs[g:2 * g]
    kn_ref, vn_ref, o_ref, qblk_ref, qq_ref, m_ref, l_ref, acc_ref = refs[2 * g:]
    p = pl.program_id(1)
    tq = q_ref.shape[1]
    n_maps, _, page = kc_refs[0].shape[1:]
    hw = 2 * DIFF_HEAD
    n_heads = n_maps // 2
    rows = 2 * tq

    @pl.when(p == 0)
    def _():
        m_ref[...] = jnp.full(m_ref.shape, -jnp.inf, F32)
        l_ref[...] = jnp.zeros(l_ref.shape, F32)
        acc_ref[...] = jnp.zeros(acc_ref.shape, F32)
        q = q_ref[0] * (DIFF_HEAD ** -0.5)
        q_rows = jnp.concatenate([q] * n_maps, axis=0)
        row_map = lax.broadcasted_iota(jnp.int32, q_rows.shape, 0) // tq
        lane_map = lax.broadcasted_iota(jnp.int32, q_rows.shape, 1) // DIFF_HEAD
        qblk_ref[...] = jnp.where(row_map == lane_map, q_rows, 0.0).astype(BF16)
        for h in range(n_heads):
            q0, q1 = _split_maps(q[:, h * hw:(h + 1) * hw])
            qq_ref[h] = jnp.concatenate([q0, q1], axis=0)

    def update(s, pv_fn, visible):
        if visible is not None:
            s = [jnp.where(visible, sj, -jnp.inf) for sj in s]
        m_old = m_ref[...]
        m_new = jnp.maximum(m_old, jnp.max(functools.reduce(jnp.maximum, s), axis=-1, keepdims=True))
        pr = [jnp.exp(sj - m_new) for sj in s]
        scale = jnp.exp(m_old - m_new)
        l_ref[...] = scale * l_ref[...] + jnp.sum(functools.reduce(jnp.add, pr), axis=-1, keepdims=True)
        acc_ref[...] = scale * acc_ref[...] + pv_fn([pj.astype(BF16) for pj in pr])
        m_ref[...] = m_new

    def page_scores(kc_ref):
        kt = kc_ref[0].reshape(n_maps * DIFF_HEAD, page).astype(BF16)
        return _dot(qblk_ref[...], kt)

    def page_pv(pr):
        out = None
        for pj, vc_ref in zip(pr, vc_refs):
            vp = vc_ref[0].reshape(page * n_heads, hw).astype(BF16)
            spread = (_dot(pj, vspread_ref[...]) * vmask_ref[...]).astype(BF16)
            d = _dot(spread, vp)
            out = d if out is None else out + d
        return out

    update([page_scores(r) for r in kc_refs], page_pv, None)

    @pl.when(p == pl.num_programs(1) - 1)
    def _():
        q_idx = lax.broadcasted_iota(jnp.int32, (n_heads * rows, page), 0) % tq
        k_idx = lax.broadcasted_iota(jnp.int32, (n_heads * rows, page), 1)
        pad = jnp.zeros((page - tq, hw), BF16)

        def head_block(ref, h):
            return jnp.concatenate([ref[0, :, h * hw:(h + 1) * hw].astype(BF16), pad], axis=0)

        s_new = jnp.concatenate([_dot_nt(qq_ref[h], head_block(kn_ref, h)) for h in range(n_heads)], axis=0)

        def new_pv(pr):
            return jnp.concatenate([_dot(pr[0][h * rows:(h + 1) * rows, :], head_block(vn_ref, h))
                                    for h in range(n_heads)], axis=0)

        update([s_new], new_pv, k_idx <= q_idx)
        lam = _lambda_value(lam_ref)
        on = acc_ref[...] / l_ref[...]
        for h in range(n_heads):
            o = on[h * rows:h * rows + tq, :] - lam * on[h * rows + tq:(h + 1) * rows, :]
            o_ref[0, :, h * hw:(h + 1) * hw] = _sub_ln(o, subln_ref[...])


def _attn_decode(q, cache_k, cache_v, page_table, k_new, v_new, lam_vecs, subln):
    db, tq, d = q.shape
    n_pages = page_table.shape[1]
    _, page, n_maps, _ = cache_k.shape
    hw = 2 * DIFF_HEAD
    n_heads = d // hw
    g = PAGES_PER_STEP
    assert tq % SUBLANES == 0 and tq <= page and page == hw and n_pages % g == 0
    assert cache_k.shape[2:] == (2 * n_heads, DIFF_HEAD) and cache_v.shape[1:] == (page, n_heads, hw)
    cache_k = jnp.transpose(cache_k, (0, 2, 3, 1))

    row_map = jnp.arange(n_maps * tq)[:, None] // tq
    v_col = jnp.arange(page * n_heads)[None, :]
    vspread = (jnp.arange(page)[:, None] == v_col // n_heads).astype(BF16)
    vmask = (row_map // 2 == v_col % n_heads).astype(F32)
    consts = [lam_vecs, subln, vspread, vmask]

    def page_map(j):
        return lambda bi, p, pt: (pt[bi * n_pages + p * g + j], 0, 0, 0)

    def batch_map(bi, p, pt):
        return (bi, 0, 0)

    k_specs = [pl.BlockSpec((1,) + cache_k.shape[1:], page_map(j)) for j in range(g)]
    v_specs = [pl.BlockSpec((1,) + cache_v.shape[1:], page_map(j)) for j in range(g)]
    stat = pltpu.VMEM((n_maps * tq, hw), F32)
    return pl.pallas_call(
        _attn_decode_kernel,
        grid_spec=pltpu.PrefetchScalarGridSpec(
            num_scalar_prefetch=1,
            grid=(db, n_pages // g),
            in_specs=[
                *[pl.BlockSpec(c.shape, lambda bi, p, pt: (0, 0)) for c in consts],
                pl.BlockSpec((1, tq, d), batch_map),
                *k_specs,
                *v_specs,
                pl.BlockSpec((1, tq, d), batch_map),
                pl.BlockSpec((1, tq, d), batch_map),
            ],
            out_specs=pl.BlockSpec((1, tq, d), batch_map),
            scratch_shapes=[pltpu.VMEM((n_maps * tq, d), BF16),
                            pltpu.VMEM((n_heads, 2 * tq, hw), BF16), stat, stat, stat],
        ),
        out_shape=jax.ShapeDtypeStruct((db, tq, d), F32),
        compiler_params=_cparams("parallel", "arbitrary"),
        name="attn_decode",
    )(page_table.reshape(-1), *consts, q, *([cache_k] * g), *([cache_v] * g), k_new, v_new)


def _to_head_batch_lanes(z, b, seq, heads):
    z = z.reshape(b, seq, heads, RWKV_HEAD)
    return jnp.transpose(z, (1, 3, 2, 0)).reshape(seq, RWKV_HEAD, heads * b)


def _from_head_batch_lanes(z, b, seq, heads):
    z = z.reshape(seq, RWKV_HEAD, heads, b)
    return jnp.transpose(z, (3, 0, 2, 1)).reshape(b * seq, heads * RWKV_HEAD)


def _run(x, wkv_in, shift_in, past, pr):
    b, t, d = x.shape
    n = b * t
    heads = d // RWKV_HEAD
    xf = x.reshape(n, d)

    x_prev = jnp.concatenate([shift_in[:, None, :], x[:, :-1]], axis=1).reshape(n, d)
    time_major = b != LANES
    r, w, k, v, a, bb, g = _rwkv_pre(xf, x_prev, pr["rwkv"], b, t, time_major)
    if not time_major:
        scan_ops = [_to_head_batch_lanes(z, b, t, heads) for z in (r, w, k, v, a, bb)]
        o_l, s_fin = _rwkv_scan_native_state(*scan_ops, wkv_in.reshape(b, -1))
        o = _from_head_batch_lanes(o_l, b, t, heads)
        new_wkv = s_fin.reshape(wkv_in.shape)[:, None]
    else:
        s0 = jnp.transpose(wkv_in.reshape(b, heads // 2, 2, RWKV_HEAD, RWKV_HEAD), (4, 3, 2, 1, 0))
        o3, s_fin = _rwkv_scan(*[z.reshape(t, b, d) for z in (r, w, k, v, a, bb)],
                               s0.reshape(RWKV_HEAD, RWKV_HEAD, heads * b))
        o = o3.reshape(t, b * d)
        s_fin = s_fin.reshape(RWKV_HEAD, RWKV_HEAD, 2, heads // 2, b)
        new_wkv = jnp.transpose(s_fin, (4, 3, 2, 1, 0)).reshape(wkv_in.shape)[:, None]
    new_shift = x[:, -1][:, None]
    x1 = _rwkv_post(o, r, k, v, g, xf, pr["rwkv"], pr["post_ln"][0][0], t, time_major)
    x2 = _moe_ln(x1, pr["post_ln"][0][1], pr["router_wt"], pr["router_b"], *pr["moe"][0])

    k_new, v_new, q = _proj(x2, [pr["w_k_shared"], pr["w_v_shared"], pr["diff_w_q"]], "kv_q_proj")
    k3, v3, q3 = (z.reshape(b, t, d) for z in (k_new, v_new, q))
    if past is None:
        att = _attn_prompt(q3, k3, v3, pr["diff_lambda"], pr["diff_subln"])
    else:
        cache_k, cache_v, page_table = past
        att = _attn_decode(q3, cache_k, cache_v, page_table, k3, v3, pr["diff_lambda"], pr["diff_subln"])
    x3 = _proj_ln(att.reshape(n, d), x2, pr["diff_w_out"], pr["post_ln"][1][0])
    x4 = _moe_ln(x3, pr["post_ln"][1][1], pr["router_wt"], pr["router_b"], *pr["moe"][1])

    n_diff = d // (2 * DIFF_HEAD)
    return (x4.reshape(b, t, d), k_new.reshape(b, t, 2 * n_diff, DIFF_HEAD),
            v_new.reshape(b, t, n_diff, 2 * DIFF_HEAD), new_wkv, new_shift)


def kernel(x_prompt, x_sample, cache_k, cache_v, page_table, state_wkv, state_shift, meta_tokens,
           rwkv_mix, rwkv_w_rkv, rwkv_decay_w0, rwkv_decay_w1, rwkv_decay_w2, rwkv_a_w0, rwkv_a_w1,
           rwkv_a_w2, rwkv_g_w1, rwkv_g_w2, rwkv_kk_ka, rwkv_r_k, rwkv_lnx, rwkv_w_out, w_kv_shared,
           diff_w_q, diff_lambda, diff_subln, diff_w_out, router_w, router_b, moe_w_gate, moe_w_up,
           moe_w_down, post_ln):
    d = x_prompt.shape[-1]
    assert rwkv_mix.shape[0] == N_A_LAYERS and moe_w_gate.shape[0] == DEPTH
    ind, ind_t = _head_indicator(d, RWKV_HEAD)
    qk_width = diff_w_q.shape[-1]
    pr = {
        "rwkv": {
            "mix": rwkv_mix[0], "w_r": rwkv_w_rkv[0, 0].astype(BF16), "w_k": rwkv_w_rkv[0, 1].astype(BF16),
            "w_v": rwkv_w_rkv[0, 2].astype(BF16), "dw0": rwkv_decay_w0, "dw1": rwkv_decay_w1[0].astype(BF16),
            "dw2": rwkv_decay_w2[0].astype(BF16), "aw0": rwkv_a_w0, "aw1": rwkv_a_w1[0].astype(BF16),
            "aw2": rwkv_a_w2[0].astype(BF16), "gw1": rwkv_g_w1[0].astype(BF16),
            "gw2": rwkv_g_w2[0].astype(BF16), "kk_ka": rwkv_kk_ka[0], "r_k": rwkv_r_k[0].reshape(1, d),
            "lnx": rwkv_lnx[0], "w_out": rwkv_w_out[0].astype(BF16), "ind": ind, "ind_t": ind_t,
        },
        "post_ln": post_ln,
        "router_wt": router_w.T,
        "router_b": router_b.reshape(N_EXPERTS, 1),
        "moe": [(moe_w_gate[l].astype(BF16), moe_w_up[l].astype(BF16), moe_w_down[l].astype(BF16))
                for l in range(DEPTH)],
        "w_k_shared": w_kv_shared[:, :qk_width].astype(BF16),
        "w_v_shared": w_kv_shared[:, qk_width:].astype(BF16),
        "diff_w_q": diff_w_q[0].astype(BF16),
        "diff_lambda": diff_lambda[0],
        "diff_subln": diff_subln[0].reshape(1, -1),
        "diff_w_out": diff_w_out[0].astype(BF16),
    }

    bp = x_prompt.shape[0]
    n_meta = meta_tokens.shape[0]
    meta = jnp.broadcast_to(meta_tokens[None].astype(x_prompt.dtype), (bp, n_meta, d))
    xp = jnp.concatenate([meta, x_prompt], axis=1)
    wkv0 = jnp.zeros((bp,) + state_wkv.shape[2:], x_prompt.dtype)
    shift0 = jnp.zeros((bp, d), x_prompt.dtype)
    yp, k_p, v_p, wkv_p, shift_p = _run(xp, wkv0, shift0, None, pr)

    past = (cache_k, cache_v, page_table)
    ys, k_s, v_s, wkv_s, shift_s = _run(x_sample, state_wkv[:, 0], state_shift[:, 0], past, pr)
    return (yp[:, n_meta:], ys, k_p, v_p, wkv_p, shift_p, k_s, v_s, wkv_s, shift_s)
```

```python
import functools
import math

import jax
import jax.numpy as jnp
from jax import lax
from jax.experimental import pallas as pl
from jax.experimental.pallas import tpu as pltpu

F32 = jnp.float32
BF16 = jnp.bfloat16

DEPTH = 2
N_A_LAYERS = 1
RWKV_HEAD = 64
DIFF_HEAD = 64
LNX_EPS = 64e-5
SUBLN_EPS = 1e-5
LN_EPS = 1e-5
N_EXPERTS = 16
N_GROUPS = 4
EXPERTS_PER_GROUP = N_EXPERTS // N_GROUPS
TOP_K = 2
DEEPNORM_ALPHA = (2 * DEPTH) ** 0.25
LAMBDA_INIT = 0.8 - 0.6 * math.exp(-0.3 * 1)

LANES = 128
SUBLANES = 8
VMEM_LIMIT_BYTES = 56 * 1024 * 1024


def _cparams(*sem):
    return pltpu.CompilerParams(dimension_semantics=sem, vmem_limit_bytes=VMEM_LIMIT_BYTES)


def _row_tile(n, candidates):
    for c in candidates:
        if n % c == 0:
            return c
    raise ValueError(f"no row tile for {n} in {candidates}")


def _full(shape):
    zeros = (0,) * len(shape)
    return pl.BlockSpec(shape, lambda *_: zeros)


def _dot(a, b):
    return jnp.dot(a, b, preferred_element_type=F32)


def _dot_nt(a, b):
    return lax.dot_general(a, b, (((1,), (1,)), ((), ())), preferred_element_type=F32)


def _split_dot(x, w_bf16):
    hi = x.astype(BF16)
    lo = (x - hi.astype(F32)).astype(BF16)
    return _dot(hi, w_bf16) + _dot(lo, w_bf16)


def _layernorm(z, g, b):
    mu = jnp.mean(z, axis=-1, keepdims=True)
    zc = z - mu
    var = jnp.mean(zc * zc, axis=-1, keepdims=True)
    return zc * lax.rsqrt(var + LN_EPS) * g + b


def _head_indicator(d, head):
    n_heads = d // head
    assert n_heads <= LANES
    ind = (jnp.arange(d)[:, None] // head == jnp.arange(LANES)[None, :]).astype(BF16)
    return ind, ind.T


def _rwkv_pre_kernel(x_ref, xp_ref, mix_ref, wr_ref, wk_ref, wv_ref, dw0_ref, dw1_ref, dw2_ref,
                     aw0_ref, aw1_ref, aw2_ref, gw1_ref, gw2_ref, kkka_ref, ind_ref, indt_ref,
                     r_ref, w_ref, k_ref, v_ref, a_ref, b_ref, g_ref):
    x = x_ref[...]
    xx = xp_ref[...] - x

    def mixed(i):
        return (x + xx * mix_ref[i:i + 1, :]).astype(BF16)

    r = _dot(mixed(0), wr_ref[...])
    k = _dot(mixed(2), wk_ref[...])
    v = _dot(mixed(3), wv_ref[...])
    lw = jnp.tanh(_dot(mixed(1), dw1_ref[...])).astype(BF16)
    z = -(dw0_ref[...] + _dot(lw, dw2_ref[...]))
    softplus = jnp.maximum(z, 0.0) + jnp.log(1.0 + jnp.exp(-jnp.abs(z)))
    w_log = -softplus - 0.5
    decay = jnp.exp(-jnp.exp(w_log))
    la = _dot(mixed(4), aw1_ref[...]).astype(BF16)
    a = jax.nn.sigmoid(aw0_ref[...] + _dot(la, aw2_ref[...]))
    lg = jax.nn.sigmoid(_dot(mixed(5), gw1_ref[...])).astype(BF16)
    g = _dot(lg, gw2_ref[...])

    kk = k * kkka_ref[0:1, :]
    ss = _split_dot(_split_dot(kk * kk, ind_ref[...]), indt_ref[...])
    kk = kk / jnp.maximum(jnp.sqrt(ss), 1e-12)
    k = k * (1.0 + (a - 1.0) * kkka_ref[1:2, :])

    r_ref[...] = r
    w_ref[...] = decay
    k_ref[...] = k
    v_ref[...] = v
    a_ref[...] = -kk
    b_ref[...] = kk * a
    g_ref[...] = g


def _time_major_spec(tm, d, tiles_per_seq):
    return pl.BlockSpec((tm, d), lambda i: (i % tiles_per_seq, i // tiles_per_seq))


def _rwkv_pre(x, x_prev, p, b, t, time_major):
    n, d = x.shape
    tm = _row_tile(t, (344, 256, 128, 64, 8)) if time_major else _row_tile(n, (192, 128, 64, 8))
    row = pl.BlockSpec((tm, d), lambda i: (i, 0))
    seq_spec = _time_major_spec(tm, d, t // tm) if time_major else row
    seq_shape = (t, b * d) if time_major else (n, d)
    weights = [p["mix"], p["w_r"], p["w_k"], p["w_v"], p["dw0"], p["dw1"], p["dw2"], p["aw0"],
               p["aw1"], p["aw2"], p["gw1"], p["gw2"], p["kk_ka"], p["ind"], p["ind_t"]]
    return pl.pallas_call(
        _rwkv_pre_kernel,
        grid=(n // tm,),
        in_specs=[row, row] + [_full(w.shape) for w in weights],
        out_specs=[seq_spec] * 6 + [row],
        out_shape=[jax.ShapeDtypeStruct(seq_shape, F32)] * 6 + [jax.ShapeDtypeStruct((n, d), F32)],
        compiler_params=_cparams("parallel"),
        name="rwkv_pre",
    )(x, x_prev, *weights)


SCAN_VALUE_SPLIT = 2


def _scan_groups(s_ref):
    nv = s_ref.shape[1] // SCAN_VALUE_SPLIT
    return nv, range(0, s_ref.shape[1], nv)


def _state_times(s_ref, vec_ref):
    nv, groups = _scan_groups(s_ref)
    out = []
    for lo in groups:
        acc = [jnp.zeros((nv, LANES), F32), jnp.zeros((nv, LANES), F32)]
        for j in range(s_ref.shape[0]):
            acc[j % 2] = acc[j % 2] + s_ref[j, lo:lo + nv, :] * vec_ref[pl.ds(j, 1), :]
        out.append(acc[0] + acc[1])
    return tuple(out)


def _scan_step(s_ref, r_ref, w_ref, k_ref, v_ref, b_ref, a_next_ref, o_ref, sa_all):
    nv, groups = _scan_groups(s_ref)
    sa_next = []
    for g, lo in enumerate(groups):
        vt = v_ref[lo:lo + nv, :]
        sa = sa_all[g]
        ot = [jnp.zeros_like(vt), jnp.zeros_like(vt)]
        sn = [jnp.zeros_like(vt), jnp.zeros_like(vt)]
        for j in range(s_ref.shape[0]):
            s = (s_ref[j, lo:lo + nv, :] * w_ref[pl.ds(j, 1), :] + sa * b_ref[pl.ds(j, 1), :]
                 + vt * k_ref[pl.ds(j, 1), :])
            s_ref[j, lo:lo + nv, :] = s
            ot[j % 2] = ot[j % 2] + s * r_ref[pl.ds(j, 1), :]
            sn[j % 2] = sn[j % 2] + s * a_next_ref[pl.ds(j, 1), :]
        o_ref[lo:lo + nv, :] = ot[0] + ot[1]
        sa_next.append(sn[0] + sn[1])
    return tuple(sa_next)


def _rwkv_scan_kernel(r_ref, w_ref, k_ref, v_ref, a_ref, b_ref, s0_ref, o_ref, sout_ref, s_ref,
                      buf_a, buf_b, obuf_a, obuf_b):
    tc = pl.program_id(0)
    steps, nb, d = r_ref.shape
    n_chunks = d // LANES
    half = LANES // 2
    srcs = (r_ref, w_ref, k_ref, v_ref, a_ref, b_ref)

    @pl.when(tc == 0)
    def _():
        s_ref[...] = s0_ref[...]

    def to_lanes(t, buf):
        for i_op, src in enumerate(srcs):
            x = src[t]
            rows = jnp.concatenate([x[:, i * LANES:(i + 1) * LANES] for i in range(n_chunks)], axis=0)
            cols = rows.T
            buf[i_op] = jnp.concatenate([cols[:half], cols[half:]], axis=1)

    def from_lanes(obuf, t):
        y = obuf[...]
        rows = jnp.concatenate([y[:, :half], y[:, half:]], axis=0).T
        for i in range(n_chunks):
            o_ref[t, :, i * LANES:(i + 1) * LANES] = rows[i * nb:(i + 1) * nb, :]

    def step(cur, nxt, obuf, sa):
        r_b, w_b, k_b, v_b, _, b_b = (cur.at[i] for i in range(6))
        return _scan_step(s_ref, r_b, w_b, k_b, v_b, b_b, nxt.at[4], obuf, sa)

    def pair(i, sa):
        t = 2 * i
        to_lanes(t + 1, buf_b)
        from_lanes(obuf_b, jnp.maximum(t - 1, 0))
        sa = step(buf_a, buf_b, obuf_a, sa)
        to_lanes(jnp.minimum(t + 2, steps - 1), buf_a)
        from_lanes(obuf_a, t)
        return step(buf_b, buf_a, obuf_b, sa)

    to_lanes(0, buf_a)
    obuf_b[...] = jnp.zeros(obuf_b.shape, F32)
    lax.fori_loop(0, steps // 2, pair, _state_times(s_ref, buf_a.at[4]))
    from_lanes(obuf_b, steps - 1)

    @pl.when(tc == pl.num_programs(0) - 1)
    def _():
        sout_ref[...] = s_ref[...]


def _rwkv_scan_native_state_kernel(r_ref, w_ref, k_ref, v_ref, a_ref, b_ref, s0_ref, o_ref, sout_ref,
                                   s_ref, tmp_ref):
    steps = r_ref.shape[0]
    nk = s_ref.shape[0]
    tmp_ref[...] = s0_ref[...].T
    for j in range(nk):
        s_ref[j] = tmp_ref[pl.ds(j, nk, stride=nk), :]

    def step(t, sa):
        return _scan_step(s_ref, r_ref.at[t], w_ref.at[t], k_ref.at[t], v_ref.at[t], b_ref.at[t],
                          a_ref.at[jnp.minimum(t + 1, steps - 1)], o_ref.at[t], sa)

    lax.fori_loop(0, steps, step, _state_times(s_ref, a_ref.at[0]))
    for j in range(nk):
        tmp_ref[pl.ds(j, nk, stride=nk), :] = s_ref[j]
    sout_ref[...] = tmp_ref[...].T


def _rwkv_scan_native_state(r, w, k, v, a, b, state):
    t, nk, l = r.shape
    nb, width = state.shape
    assert nb == LANES and width == (l // LANES) * nk * nk
    seq = pl.BlockSpec((t, nk, LANES), lambda h: (0, 0, h))
    st = pl.BlockSpec((LANES, nk * nk), lambda h: (0, h))
    return pl.pallas_call(
        _rwkv_scan_native_state_kernel,
        grid=(l // LANES,),
        in_specs=[seq] * 6 + [st],
        out_specs=[seq, st],
        out_shape=[jax.ShapeDtypeStruct((t, nk, l), F32), jax.ShapeDtypeStruct(state.shape, F32)],
        scratch_shapes=[pltpu.VMEM((nk, nk, LANES), F32), pltpu.VMEM((nk * nk, LANES), F32)],
        compiler_params=_cparams("parallel"),
        name="rwkv_scan_native_state",
    )(r, w, k, v, a, b, state)


def _rwkv_scan(r, w, k, v, a, b, s0):
    t, nb, d = r.shape
    nk = RWKV_HEAD
    assert nb * (d // LANES) == LANES // 2 and d // nk == 2 * (d // LANES)
    steps = _row_tile(t, (48, 16, 8, 4, 2))
    seq = pl.BlockSpec((steps, nb, d), lambda c: (c, 0, 0))
    st = pl.BlockSpec((nk, nk, LANES), lambda c: (0, 0, 0))
    operands = pltpu.VMEM((6, nk, LANES), F32)
    readout = pltpu.VMEM((nk, LANES), F32)
    return pl.pallas_call(
        _rwkv_scan_kernel,
        grid=(t // steps,),
        in_specs=[seq] * 6 + [st],
        out_specs=[seq, st],
        out_shape=[jax.ShapeDtypeStruct((t, nb, d), F32), jax.ShapeDtypeStruct((nk, nk, LANES), F32)],
        scratch_shapes=[pltpu.VMEM((nk, nk, LANES), F32), operands, operands, readout, readout],
        compiler_params=_cparams("arbitrary"),
        name="rwkv_scan",
    )(r, w, k, v, a, b, s0)


def _rwkv_post_kernel(o_ref, r_ref, k_ref, v_ref, g_ref, x_ref, lnx_ref, rk_ref, ind_ref, indt_ref,
                      wout_ref, ln_ref, out_ref):
    ind = ind_ref[...]
    ind_t = indt_ref[...]
    inv_head = 1.0 / RWKV_HEAD
    o = o_ref[...]
    mu = _split_dot(_split_dot(o, ind) * inv_head, ind_t)
    oc = o - mu
    var = _split_dot(oc * oc, ind) * inv_head
    rstd = _split_dot(lax.rsqrt(var + LNX_EPS), ind_t)
    on = oc * rstd * lnx_ref[0:1, :] + lnx_ref[1:2, :]
    v = v_ref[...]
    bonus = _split_dot(_split_dot(r_ref[...] * k_ref[...] * rk_ref[...], ind), ind_t)
    on = on + bonus * v
    h = _dot((on * g_ref[...]).astype(BF16), wout_ref[...])
    out_ref[...] = _layernorm(DEEPNORM_ALPHA * x_ref[...] + h, ln_ref[0:1, :], ln_ref[1:2, :])


def _rwkv_post(o, r, k, v, g, x, p, ln, t, time_major):
    n, d = x.shape
    tm = _row_tile(t, (344, 256, 128, 64, 8)) if time_major else _row_tile(n, (384, 256, 128, 64, 8))
    row = pl.BlockSpec((tm, d), lambda i: (i, 0))
    seq_spec = _time_major_spec(tm, d, t // tm) if time_major else row
    weights = [p["lnx"], p["r_k"], p["ind"], p["ind_t"], p["w_out"], ln]
    return pl.pallas_call(
        _rwkv_post_kernel,
        grid=(n // tm,),
        in_specs=[seq_spec] * 4 + [row] * 2 + [_full(w.shape) for w in weights],
        out_specs=row,
        out_shape=jax.ShapeDtypeStruct((n, d), F32),
        compiler_params=_cparams("parallel"),
        name="rwkv_post",
    )(o, r, k, v, g, x, *weights)


def _proj_kernel(*refs, n_out):
    x_ref = refs[0]
    w_refs = refs[1:1 + n_out]
    out_refs = refs[1 + n_out:]
    x = x_ref[...].astype(BF16)
    for w_ref, out_ref in zip(w_refs, out_refs):
        out_ref[...] = _dot(x, w_ref[...])


def _proj(x, ws, name):
    n, d = x.shape
    tm = _row_tile(n, (384, 256, 128, 64, 8))
    return pl.pallas_call(
        functools.partial(_proj_kernel, n_out=len(ws)),
        grid=(n // tm,),
        in_specs=[pl.BlockSpec((tm, d), lambda i: (i, 0))] + [_full(w.shape) for w in ws],
        out_specs=[pl.BlockSpec((tm, w.shape[1]), lambda i: (i, 0)) for w in ws],
        out_shape=[jax.ShapeDtypeStruct((n, w.shape[1]), F32) for w in ws],
        compiler_params=_cparams("parallel"),
        name=name,
    )(x, *ws)


def _proj_ln_kernel(a_ref, x_ref, w_ref, ln_ref, out_ref):
    h = _dot(a_ref[...].astype(BF16), w_ref[...])
    out_ref[...] = _layernorm(DEEPNORM_ALPHA * x_ref[...] + h, ln_ref[0:1, :], ln_ref[1:2, :])


def _proj_ln(a, x, w, ln):
    n, d = x.shape
    tm = _row_tile(n, (384, 256, 128, 64, 8))
    row = pl.BlockSpec((tm, d), lambda i: (i, 0))
    return pl.pallas_call(
        _proj_ln_kernel,
        grid=(n // tm,),
        in_specs=[pl.BlockSpec((tm, a.shape[1]), lambda i: (i, 0)), row, _full(w.shape), _full(ln.shape)],
        out_specs=row,
        out_shape=jax.ShapeDtypeStruct((n, d), F32),
        compiler_params=_cparams("parallel"),
        name="proj_ln",
    )(a, x, w, ln)


MOE_TILE = 512


def _router_kernel(x_ref, wt_ref, bias_ref, tri_ref, eidx_ref, wts_ref, rank_ref, cnt_ref, run_ref):
    step = pl.program_id(0)

    @pl.when(step == 0)
    def _():
        run_ref[...] = jnp.zeros(run_ref.shape, F32)

    x = x_ref[...]
    x_hi = x.astype(BF16)
    x_lo = (x - x_hi.astype(F32)).astype(BF16)
    wt = wt_ref[...]
    wt_hi = wt.astype(BF16)
    wt_lo = (wt - wt_hi.astype(F32)).astype(BF16)
    logits = _dot_nt(wt_hi, x_hi) + _dot_nt(wt_hi, x_lo) + _dot_nt(wt_lo, x_hi)
    s = jax.nn.sigmoid(logits)
    sel = s + bias_ref[...]
    s_rows = [s[e:e + 1, :] for e in range(N_EXPERTS)]
    sel_rows = [sel[e:e + 1, :] for e in range(N_EXPERTS)]

    best_grp = None
    best_idx = None
    for gi in range(N_GROUPS):
        rows = sel_rows[gi * EXPERTS_PER_GROUP:(gi + 1) * EXPERTS_PER_GROUP]
        score = None
        for i in range(EXPERTS_PER_GROUP):
            for j in range(i + 1, EXPERTS_PER_GROUP):
                pair = rows[i] + rows[j]
                score = pair if score is None else jnp.maximum(score, pair)
        if best_grp is None:
            best_grp = score
            best_idx = jnp.zeros(score.shape, jnp.int32)
        else:
            better = score > best_grp
            best_grp = jnp.where(better, score, best_grp)
            best_idx = jnp.where(better, gi, best_idx)

    neg_inf = jnp.float32(-jnp.inf)
    masked = [jnp.where(best_idx == e // EXPERTS_PER_GROUP, sel_rows[e], neg_inf)
              for e in range(N_EXPERTS)]

    def argmax_rows(exclude):
        top = jnp.full(masked[0].shape, neg_inf)
        idx = jnp.full(masked[0].shape, -1, jnp.int32)
        val = jnp.zeros(masked[0].shape, F32)
        for e in range(N_EXPERTS):
            cand = masked[e] if exclude is None else jnp.where(exclude == e, neg_inf, masked[e])
            better = cand > top
            top = jnp.where(better, cand, top)
            idx = jnp.where(better, e, idx)
            val = jnp.where(better, s_rows[e], val)
        return idx, val

    i1, w1 = argmax_rows(None)
    i2, w2 = argmax_rows(i1)
    tot = w1 + w2
    eidx_ref[0:1, :] = i1
    eidx_ref[1:2, :] = i2
    wts_ref[0:1, :] = w1 / tot
    wts_ref[1:2, :] = w2 / tot

    e_iota = lax.broadcasted_iota(jnp.int32, logits.shape, 0)
    run = run_ref[...]
    for slot, idx in enumerate((i1, i2)):
        onehot = jnp.where(e_iota == idx, 1.0, 0.0)
        prefix = _dot(onehot.astype(BF16), tri_ref[...])
        rank = jnp.sum(onehot * (run[:, 0:1] + prefix - 1.0), axis=0, keepdims=True)
        rank_ref[slot:slot + 1, :] = rank.astype(jnp.int32)
        run = run + jnp.sum(onehot, axis=-1, keepdims=True)
    run_ref[...] = run
    cnt_ref[...] = run


def _router(x, router_wt, router_b):
    n, d = x.shape
    tm = _row_tile(n, (384, 256, 128))
    tri = (jnp.arange(tm)[:, None] <= jnp.arange(tm)[None, :]).astype(BF16)
    pair = pl.BlockSpec((TOP_K, tm), lambda i: (0, i))
    return pl.pallas_call(
        _router_kernel,
        grid=(n // tm,),
        in_specs=[pl.BlockSpec((tm, d), lambda i: (i, 0)), _full(router_wt.shape), _full(router_b.shape),
                  _full(tri.shape)],
        out_specs=[pair, pair, pair, _full((N_EXPERTS, LANES))],
        out_shape=[jax.ShapeDtypeStruct((TOP_K, n), jnp.int32), jax.ShapeDtypeStruct((TOP_K, n), F32),
                   jax.ShapeDtypeStruct((TOP_K, n), jnp.int32), jax.ShapeDtypeStruct((N_EXPERTS, LANES), F32)],
        scratch_shapes=[pltpu.VMEM((N_EXPERTS, LANES), F32)],
        compiler_params=_cparams("arbitrary"),
        name="router",
    )(x, router_wt, router_b, tri)


DMA_UNROLL = 8


def _dispatch_kernel(pos_ref, x_ref, init_hbm, xs_hbm, sem):
    del init_hbm
    tm = x_ref.shape[0]

    def row_copy(slot, r):
        return pltpu.make_async_copy(x_ref.at[pl.ds(r, 1)], xs_hbm.at[pl.ds(pos_ref[0, slot, r], 1)], sem)

    def start(r, c):
        for slot in range(TOP_K):
            row_copy(slot, r).start(priority=slot)
        return c

    def wait(r, c):
        for slot in range(TOP_K):
            row_copy(slot, r).wait()
        return c

    lax.fori_loop(0, tm, start, 0, unroll=DMA_UNROLL)
    lax.fori_loop(0, tm, wait, 0, unroll=DMA_UNROLL)


def _dispatch(x, pos_blocks, n_rows):
    n, d = x.shape
    nt, _, tm = pos_blocks.shape
    return pl.pallas_call(
        _dispatch_kernel,
        grid=(nt,),
        in_specs=[pl.BlockSpec((1, TOP_K, tm), lambda i: (i, 0, 0), memory_space=pltpu.SMEM),
                  pl.BlockSpec((tm, d), lambda i: (i, 0)),
                  pl.BlockSpec(memory_space=pl.ANY)],
        out_specs=pl.BlockSpec(memory_space=pl.ANY),
        out_shape=jax.ShapeDtypeStruct((n_rows, d), F32),
        scratch_shapes=[pltpu.SemaphoreType.DMA(())],
        input_output_aliases={2: 0},
        compiler_params=_cparams("arbitrary"),
        name="moe_dispatch",
    )(pos_blocks, x, jnp.zeros((n_rows, d), F32))


def _experts_kernel(tile_e_ref, n_used_ref, xs_ref, wg_ref, wu_ref, wd_ref, ys_ref):
    del tile_e_ref
    i = pl.program_id(0)

    @pl.when(i < n_used_ref[0])
    def _():
        x = xs_ref[...].astype(BF16)
        hg = _dot(x, wg_ref[0])
        hu = _dot(x, wu_ref[0])
        h = (hg * jax.nn.sigmoid(hg) * hu).astype(BF16)
        ys_ref[...] = _dot(h, wd_ref[0])

    @pl.when(i >= n_used_ref[0])
    def _():
        ys_ref[...] = jnp.zeros(ys_ref.shape, F32)


def _experts(xs, tile_e, n_used, wg, wu, wd):
    n_rows, d = xs.shape
    d_exp = wg.shape[2]
    tm = MOE_TILE

    def w_map(i, te, nu):
        return (te[i], 0, 0)

    return pl.pallas_call(
        _experts_kernel,
        grid_spec=pltpu.PrefetchScalarGridSpec(
            num_scalar_prefetch=2,
            grid=(n_rows // tm,),
            in_specs=[
                pl.BlockSpec((tm, d), lambda i, te, nu: (jnp.minimum(i, nu[0] - 1), 0)),
                pl.BlockSpec((1, d, d_exp), w_map),
                pl.BlockSpec((1, d, d_exp), w_map),
                pl.BlockSpec((1, d_exp, d), w_map),
            ],
            out_specs=pl.BlockSpec((tm, d), lambda i, te, nu: (i, 0)),
        ),
        out_shape=jax.ShapeDtypeStruct((n_rows, d), F32),
        compiler_params=_cparams("arbitrary"),
        name="moe_experts",
    )(tile_e, n_used, xs, wg, wu, wd)


def _combine_ln_kernel(pos_ref, wts_ref, x_ref, ln_ref, ys_hbm, out_ref, ybuf, sem):
    tm = x_ref.shape[0]

    def row_copy(slot, r):
        return pltpu.make_async_copy(ys_hbm.at[pl.ds(pos_ref[0, slot, r], 1)], ybuf.at[slot, pl.ds(r, 1)], sem)

    def start(r, c):
        for slot in range(TOP_K):
            row_copy(slot, r).start(priority=slot)
        return c

    def wait(r, c):
        for slot in range(TOP_K):
            row_copy(slot, r).wait()
        return c

    lax.fori_loop(0, tm, start, 0, unroll=DMA_UNROLL)
    lax.fori_loop(0, tm, wait, 0, unroll=DMA_UNROLL)
    y = wts_ref[:, 0:1] * ybuf[0] + wts_ref[:, 1:2] * ybuf[1]
    out_ref[...] = _layernorm(DEEPNORM_ALPHA * x_ref[...] + y, ln_ref[0:1, :], ln_ref[1:2, :])


def _combine_ln(ys, pos_blocks, wts_t, x, ln):
    n, d = x.shape
    nt, _, tm = pos_blocks.shape
    row = pl.BlockSpec((tm, d), lambda i: (i, 0))
    return pl.pallas_call(
        _combine_ln_kernel,
        grid=(nt,),
        in_specs=[pl.BlockSpec((1, TOP_K, tm), lambda i: (i, 0, 0), memory_space=pltpu.SMEM),
                  pl.BlockSpec((tm, TOP_K), lambda i: (i, 0)), row, _full(ln.shape),
                  pl.BlockSpec(memory_space=pl.ANY)],
        out_specs=row,
        out_shape=jax.ShapeDtypeStruct((n, d), F32),
        scratch_shapes=[pltpu.VMEM((TOP_K, tm, d), F32), pltpu.SemaphoreType.DMA(())],
        compiler_params=_cparams("arbitrary"),
        name="moe_combine_ln",
    )(pos_blocks, wts_t, x, ln, ys)


def _moe_ln(x, ln, router_wt, router_b, wg, wu, wd):
    n, d = x.shape
    eidx, wts, rank, cnt = _router(x, router_wt, router_b)

    n_tiles = pl.cdiv(TOP_K * n, MOE_TILE) + N_EXPERTS
    counts = cnt[:, 0].astype(jnp.int32)
    tiles_per_e = (counts + MOE_TILE - 1) // MOE_TILE
    tile_end = jnp.cumsum(tiles_per_e)
    tile_start = tile_end - tiles_per_e
    n_used = tile_end[-1:]
    experts = jnp.arange(N_EXPERTS, dtype=jnp.int32)
    base = jnp.sum(jnp.where(eidx[..., None] == experts, tile_start * MOE_TILE, 0), axis=-1)
    pos = base + rank
    tile_ids = jnp.minimum(jnp.arange(n_tiles, dtype=jnp.int32), n_used - 1)
    tile_e = jnp.sum((tile_end[None, :] <= tile_ids[:, None]).astype(jnp.int32), axis=1)
    tile_e = jnp.minimum(tile_e, N_EXPERTS - 1)

    tm = _row_tile(n, (384, 256, 128))
    pos_blocks = jnp.transpose(pos.reshape(TOP_K, n // tm, tm), (1, 0, 2))
    xs = _dispatch(x, pos_blocks, n_tiles * MOE_TILE)
    ys = _experts(xs, tile_e, n_used, wg, wu, wd)
    return _combine_ln(ys, pos_blocks, wts.T, x, ln)


def _lambda_value(lam_ref):
    lf = lam_ref[...]
    s01 = jnp.sum(lf[0:1, :] * lf[1:2, :], axis=-1, keepdims=True)
    s23 = jnp.sum(lf[2:3, :] * lf[3:4, :], axis=-1, keepdims=True)
    return jnp.exp(s01) - jnp.exp(s23) + LAMBDA_INIT


def _sub_ln(o, subln):
    o = o * lax.rsqrt(jnp.mean(o * o, axis=-1, keepdims=True) + SUBLN_EPS) * subln
    return o * (1.0 - LAMBDA_INIT)


def _split_maps(q):
    lane = lax.broadcasted_iota(jnp.int32, q.shape, 1)
    q0 = jnp.where(lane < DIFF_HEAD, q, 0.0).astype(BF16)
    q1 = jnp.where(lane >= DIFF_HEAD, q, 0.0).astype(BF16)
    return q0, q1


def _attn_prompt_kernel(lam_ref, subln_ref, q_ref, k_ref, v_ref, o_ref):
    qi = pl.program_id(2)
    tq = q_ref.shape[1]
    q0, q1 = _split_maps(q_ref[0] * (DIFF_HEAD ** -0.5))

    def chunk(j, carry, visible):
        start = pl.multiple_of(j * tq, SUBLANES)
        kc = k_ref[0, pl.ds(start, tq), :].astype(BF16)
        vc = v_ref[0, pl.ds(start, tq), :].astype(BF16)
        out = []
        for qm, (m, l, acc) in zip((q0, q1), carry):
            s = _dot_nt(qm, kc)
            if visible is not None:
                s = jnp.where(visible, s, -jnp.inf)
            m_new = jnp.maximum(m, jnp.max(s, axis=-1, keepdims=True))
            p = jnp.exp(s - m_new)
            scale = jnp.exp(m - m_new)
            l_new = scale * l + jnp.sum(p, axis=-1, keepdims=True)
            acc_new = scale * acc + _dot(p.astype(BF16), vc)
            out.append((m_new, l_new, acc_new))
        return tuple(out)

    init = (jnp.full((tq, 1), -jnp.inf, F32), jnp.zeros((tq, 1), F32), jnp.zeros((tq, 2 * DIFF_HEAD), F32))
    carry = lax.fori_loop(0, qi, functools.partial(chunk, visible=None), (init, init))
    causal = (lax.broadcasted_iota(jnp.int32, (tq, tq), 1) <= lax.broadcasted_iota(jnp.int32, (tq, tq), 0))
    (_, l0, a0), (_, l1, a1) = chunk(qi, carry, causal)
    o = a0 / l0 - _lambda_value(lam_ref) * (a1 / l1)
    o_ref[0] = _sub_ln(o, subln_ref[...])


def _attn_prompt(q, k, v, lam_vecs, subln):
    b, t, d = q.shape
    hw = 2 * DIFF_HEAD
    tq = _row_tile(t, (344, 256, 128, 64, 8))
    return pl.pallas_call(
        _attn_prompt_kernel,
        grid=(b, d // hw, t // tq),
        in_specs=[
            _full(lam_vecs.shape),
            _full(subln.shape),
            pl.BlockSpec((1, tq, hw), lambda bi, h, qi: (bi, qi, h)),
            pl.BlockSpec((1, t, hw), lambda bi, h, qi: (bi, 0, h)),
            pl.BlockSpec((1, t, hw), lambda bi, h, qi: (bi, 0, h)),
        ],
        out_specs=pl.BlockSpec((1, tq, hw), lambda bi, h, qi: (bi, qi, h)),
        out_shape=jax.ShapeDtypeStruct((b, t, d), F32),
        compiler_params=_cparams("parallel", "parallel", "arbitrary"),
        name="attn_prompt",
    )(lam_vecs, subln, q, k, v)


PAGES_PER_STEP = 8


def _attn_decode_kernel(pt_ref, lam_ref, subln_ref, vspread_ref, vmask_ref, q_ref, *refs):
    del pt_ref
    g = PAGES_PER_STEP
    kc_refs, vc_refs = refs[0:g], refs[g:2 * g]
    kn_ref, vn_ref, o_ref, qblk_ref, qq_ref, m_ref, l_ref, acc_ref = refs[2 * g:]
    p = pl.program_id(1)
    tq = q_ref.shape[1]
    n_maps, _, page = kc_refs[0].shape[1:]
    hw = 2 * DIFF_HEAD
    n_heads = n_maps // 2
    rows = 2 * tq

    @pl.when(p == 0)
    def _():
        m_ref[...] = jnp.full(m_ref.shape, -jnp.inf, F32)
        l_ref[...] = jnp.zeros(l_ref.shape, F32)
        acc_ref[...] = jnp.zeros(acc_ref.shape, F32)
        q = q_ref[0] * (DIFF_HEAD ** -0.5)
        q_rows = jnp.concatenate([q] * n_maps, axis=0)
        row_map = lax.broadcasted_iota(jnp.int32, q_rows.shape, 0) // tq
        lane_map = lax.broadcasted_iota(jnp.int32, q_rows.shape, 1) // DIFF_HEAD
        qblk_ref[...] = jnp.where(row_map == lane_map, q_rows, 0.0).astype(BF16)
        for h in range(n_heads):
            q0, q1 = _split_maps(q[:, h * hw:(h + 1) * hw])
            qq_ref[h] = jnp.concatenate([q0, q1], axis=0)

    def update(s, pv_fn, visible):
        if visible is not None:
            s = [jnp.where(visible, sj, -jnp.inf) for sj in s]
        m_old = m_ref[...]
        m_new = jnp.maximum(m_old, jnp.max(functools.reduce(jnp.maximum, s), axis=-1, keepdims=True))
        pr = [jnp.exp(sj - m_new) for sj in s]
        scale = jnp.exp(m_old - m_new)
        l_ref[...] = scale * l_ref[...] + jnp.sum(functools.reduce(jnp.add, pr), axis=-1, keepdims=True)
        acc_ref[...] = scale * acc_ref[...] + pv_fn([pj.astype(BF16) for pj in pr])
        m_ref[...] = m_new

    def page_scores(kc_ref):
        kt = kc_ref[0].reshape(n_maps * DIFF_HEAD, page).astype(BF16)
        return _dot(qblk_ref[...], kt)

    def page_pv(pr):
        out = None
        for pj, vc_ref in zip(pr, vc_refs):
            vp = vc_ref[0].reshape(page * n_heads, hw).astype(BF16)
            spread = (_dot(pj, vspread_ref[...]) * vmask_ref[...]).astype(BF16)
            d = _dot(spread, vp)
            out = d if out is None else out + d
        return out

    update([page_scores(r) for r in kc_refs], page_pv, None)

    @pl.when(p == pl.num_programs(1) - 1)
    def _():
        q_idx = lax.broadcasted_iota(jnp.int32, (n_heads * rows, page), 0) % tq
        k_idx = lax.broadcasted_iota(jnp.int32, (n_heads * rows, page), 1)
        pad = jnp.zeros((page - tq, hw), BF16)

        def head_block(ref, h):
            return jnp.concatenate([ref[0, :, h * hw:(h + 1) * hw].astype(BF16), pad], axis=0)

        s_new = jnp.concatenate([_dot_nt(qq_ref[h], head_block(kn_ref, h)) for h in range(n_heads)], axis=0)

        def new_pv(pr):
            return jnp.concatenate([_dot(pr[0][h * rows:(h + 1) * rows, :], head_block(vn_ref, h))
                                    for h in range(n_heads)], axis=0)

        update([s_new], new_pv, k_idx <= q_idx)
        lam = _lambda_value(lam_ref)
        on = acc_ref[...] / l_ref[...]
        for h in range(n_heads):
            o = on[h * rows:h * rows + tq, :] - lam * on[h * rows + tq:(h + 1) * rows, :]
            o_ref[0, :, h * hw:(h + 1) * hw] = _sub_ln(o, subln_ref[...])


def _attn_decode(q, cache_k, cache_v, page_table, k_new, v_new, lam_vecs, subln):
    db, tq, d = q.shape
    n_pages = page_table.shape[1]
    _, page, n_maps, _ = cache_k.shape
    hw = 2 * DIFF_HEAD
    n_heads = d // hw
    g = PAGES_PER_STEP
    assert tq % SUBLANES == 0 and tq <= page and page == hw and n_pages % g == 0
    assert cache_k.shape[2:] == (2 * n_heads, DIFF_HEAD) and cache_v.shape[1:] == (page, n_heads, hw)
    cache_k = jnp.transpose(cache_k, (0, 2, 3, 1))

    row_map = jnp.arange(n_maps * tq)[:, None] // tq
    v_col = jnp.arange(page * n_heads)[None, :]
    vspread = (jnp.arange(page)[:, None] == v_col // n_heads).astype(BF16)
    vmask = (row_map // 2 == v_col % n_heads).astype(F32)
    consts = [lam_vecs, subln, vspread, vmask]

    def page_map(j):
        return lambda bi, p, pt: (pt[bi * n_pages + p * g + j], 0, 0, 0)

    def batch_map(bi, p, pt):
        return (bi, 0, 0)

    k_specs = [pl.BlockSpec((1,) + cache_k.shape[1:], page_map(j)) for j in range(g)]
    v_specs = [pl.BlockSpec((1,) + cache_v.shape[1:], page_map(j)) for j in range(g)]
    stat = pltpu.VMEM((n_maps * tq, hw), F32)
    return pl.pallas_call(
        _attn_decode_kernel,
        grid_spec=pltpu.PrefetchScalarGridSpec(
            num_scalar_prefetch=1,
            grid=(db, n_pages // g),
            in_specs=[
                *[pl.BlockSpec(c.shape, lambda bi, p, pt: (0, 0)) for c in consts],
                pl.BlockSpec((1, tq, d), batch_map),
                *k_specs,
                *v_specs,
                pl.BlockSpec((1, tq, d), batch_map),
                pl.BlockSpec((1, tq, d), batch_map),
            ],
            out_specs=pl.BlockSpec((1, tq, d), batch_map),
            scratch_shapes=[pltpu.VMEM((n_maps * tq, d), BF16),
                            pltpu.VMEM((n_heads, 2 * tq, hw), BF16), stat, stat, stat],
        ),
        out_shape=jax.ShapeDtypeStruct((db, tq, d), F32),
        compiler_params=_cparams("parallel", "arbitrary"),
        name="attn_decode",
    )(page_table.reshape(-1), *consts, q, *([cache_k] * g), *([cache_v] * g), k_new, v_new)


def _to_head_batch_lanes(z, b, seq, heads):
    z = z.reshape(b, seq, heads, RWKV_HEAD)
    return jnp.transpose(z, (1, 3, 2, 0)).reshape(seq, RWKV_HEAD, heads * b)


def _from_head_batch_lanes(z, b, seq, heads):
    z = z.reshape(seq, RWKV_HEAD, heads, b)
    return jnp.transpose(z, (3, 0, 2, 1)).reshape(b * seq, heads * RWKV_HEAD)


def _run(x, wkv_in, shift_in, past, pr):
    b, t, d = x.shape
    n = b * t
    heads = d // RWKV_HEAD
    xf = x.reshape(n, d)

    x_prev = jnp.concatenate([shift_in[:, None, :], x[:, :-1]], axis=1).reshape(n, d)
    time_major = b != LANES
    r, w, k, v, a, bb, g = _rwkv_pre(xf, x_prev, pr["rwkv"], b, t, time_major)
    if not time_major:
        scan_ops = [_to_head_batch_lanes(z, b, t, heads) for z in (r, w, k, v, a, bb)]
        o_l, s_fin = _rwkv_scan_native_state(*scan_ops, wkv_in.reshape(b, -1))
        o = _from_head_batch_lanes(o_l, b, t, heads)
        new_wkv = s_fin.reshape(wkv_in.shape)[:, None]
    else:
        s0 = jnp.transpose(wkv_in.reshape(b, heads // 2, 2, RWKV_HEAD, RWKV_HEAD), (4, 3, 2, 1, 0))
        o3, s_fin = _rwkv_scan(*[z.reshape(t, b, d) for z in (r, w, k, v, a, bb)],
                               s0.reshape(RWKV_HEAD, RWKV_HEAD, heads * b))
        o = o3.reshape(t, b * d)
        s_fin = s_fin.reshape(RWKV_HEAD, RWKV_HEAD, 2, heads // 2, b)
        new_wkv = jnp.transpose(s_fin, (4, 3, 2, 1, 0)).reshape(wkv_in.shape)[:, None]
    new_shift = x[:, -1][:, None]
    x1 = _rwkv_post(o, r, k, v, g, xf, pr["rwkv"], pr["post_ln"][0][0], t, time_major)
    x2 = _moe_ln(x1, pr["post_ln"][0][1], pr["router_wt"], pr["router_b"], *pr["moe"][0])

    k_new, v_new, q = _proj(x2, [pr["w_k_shared"], pr["w_v_shared"], pr["diff_w_q"]], "kv_q_proj")
    k3, v3, q3 = (z.reshape(b, t, d) for z in (k_new, v_new, q))
    if past is None:
        att = _attn_prompt(q3, k3, v3, pr["diff_lambda"], pr["diff_subln"])
    else:
        cache_k, cache_v, page_table = past
        att = _attn_decode(q3, cache_k, cache_v, page_table, k3, v3, pr["diff_lambda"], pr["diff_subln"])
    x3 = _proj_ln(att.reshape(n, d), x2, pr["diff_w_out"], pr["post_ln"][1][0])
    x4 = _moe_ln(x3, pr["post_ln"][1][1], pr["router_wt"], pr["router_b"], *pr["moe"][1])

    n_diff = d // (2 * DIFF_HEAD)
    return (x4.reshape(b, t, d), k_new.reshape(b, t, 2 * n_diff, DIFF_HEAD),
            v_new.reshape(b, t, n_diff, 2 * DIFF_HEAD), new_wkv, new_shift)


def kernel(x_prompt, x_sample, cache_k, cache_v, page_table, state_wkv, state_shift, meta_tokens,
           rwkv_mix, rwkv_w_rkv, rwkv_decay_w0, rwkv_decay_w1, rwkv_decay_w2, rwkv_a_w0, rwkv_a_w1,
           rwkv_a_w2, rwkv_g_w1, rwkv_g_w2, rwkv_kk_ka, rwkv_r_k, rwkv_lnx, rwkv_w_out, w_kv_shared,
           diff_w_q, diff_lambda, diff_subln, diff_w_out, router_w, router_b, moe_w_gate, moe_w_up,
           moe_w_down, post_ln):
    d = x_prompt.shape[-1]
    assert rwkv_mix.shape[0] == N_A_LAYERS and moe_w_gate.shape[0] == DEPTH
    ind, ind_t = _head_indicator(d, RWKV_HEAD)
    qk_width = diff_w_q.shape[-1]
    pr = {
        "rwkv": {
            "mix": rwkv_mix[0], "w_r": rwkv_w_rkv[0, 0].astype(BF16), "w_k": rwkv_w_rkv[0, 1].astype(BF16),
            "w_v": rwkv_w_rkv[0, 2].astype(BF16), "dw0": rwkv_decay_w0, "dw1": rwkv_decay_w1[0].astype(BF16),
            "dw2": rwkv_decay_w2[0].astype(BF16), "aw0": rwkv_a_w0, "aw1": rwkv_a_w1[0].astype(BF16),
            "aw2": rwkv_a_w2[0].astype(BF16), "gw1": rwkv_g_w1[0].astype(BF16),
            "gw2": rwkv_g_w2[0].astype(BF16), "kk_ka": rwkv_kk_ka[0], "r_k": rwkv_r_k[0].reshape(1, d),
            "lnx": rwkv_lnx[0], "w_out": rwkv_w_out[0].astype(BF16), "ind": ind, "ind_t": ind_t,
        },
        "post_ln": post_ln,
        "router_wt": router_w.T,
        "router_b": router_b.reshape(N_EXPERTS, 1),
        "moe": [(moe_w_gate[l].astype(BF16), moe_w_up[l].astype(BF16), moe_w_down[l].astype(BF16))
                for l in range(DEPTH)],
        "w_k_shared": w_kv_shared[:, :qk_width].astype(BF16),
        "w_v_shared": w_kv_shared[:, qk_width:].astype(BF16),
        "diff_w_q": diff_w_q[0].astype(BF16),
        "diff_lambda": diff_lambda[0],
        "diff_subln": diff_subln[0].reshape(1, -1),
        "diff_w_out": diff_w_out[0].astype(BF16),
    }

    bp = x_prompt.shape[0]
    n_meta = meta_tokens.shape[0]
    meta = jnp.broadcast_to(meta_tokens[None].astype(x_prompt.dtype), (bp, n_meta, d))
    xp = jnp.concatenate([meta, x_prompt], axis=1)
    wkv0 = jnp.zeros((bp,) + state_wkv.shape[2:], x_prompt.dtype)
    shift0 = jnp.zeros((bp, d), x_prompt.dtype)
    yp, k_p, v_p, wkv_p, shift_p = _run(xp, wkv0, shift0, None, pr)

    past = (cache_k, cache_v, page_table)
    ys, k_s, v_s, wkv_s, shift_s = _run(x_sample, state_wkv[:, 0], state_shift[:, 0], past, pr)
    return (yp[:, n_meta:], ys, k_p, v_p, wkv_p, shift_p, k_s, v_s, wkv_s, shift_s)
```
